```python
import math
import jax, jax.numpy as jnp
from jax import lax
import numpy as np

D_MODEL = 1024
BATCH = 8
SEQ = 4096
DEPTH = 1
DEC_BATCH = 8
DEC_SEQ = 32
PAST_LEN = 2048

CHUNK = 64
QBLOCK = CHUNK
N_HEADS = 8
HEAD_DIM = 64
N_KV_HEADS = 2
GQA_GROUP = N_HEADS // N_KV_HEADS
ATTN_WIDTH = N_HEADS * HEAD_DIM
N_IDX_HEADS = 8
IDX_DIM = 64
TOPK_MAX = 256
NUM_BUCKETS = 32
MAX_DISTANCE = 128
RNN_WIDTH = D_MODEL // 2
RNN_BLOCKS = 8
RNN_BLOCK_DIM = RNN_WIDTH // RNN_BLOCKS
RNN_CONV = 4
RGLRU_C = 8.0
MIX_WIDTH = ATTN_WIDTH + RNN_WIDTH
D_FF = 2816
FFN_CONV = 3
EPS = 1e-6
Q_COLS = ATTN_WIDTH
KV_COLS = N_KV_HEADS * HEAD_DIM
QI_COLS = N_IDX_HEADS * IDX_DIM
KI_COLS = IDX_DIM
WI_COLS = N_IDX_HEADS
IN_COLS = Q_COLS + 2 * KV_COLS + QI_COLS + KI_COLS + WI_COLS + 2 * RNN_WIDTH

kernel_name = 'hybrid_dsa_rglru_convffn_stream_step'


def rmsnorm(x, g):
    xf = x.astype(jnp.float32)
    y = xf * lax.rsqrt(jnp.mean(xf * xf, axis=-1, keepdims=True) + EPS)
    return (y * g.astype(jnp.float32)).astype(x.dtype)


def causal_dwconv(x, past, w, b):
    width = w.shape[0]
    t = x.shape[1]
    xp = jnp.concatenate([past, x], axis=1)
    out = b + w[0] * xp[:, 0:t]
    for j in range(1, width):
        out = out + w[j] * xp[:, j:j + t]
    return out, xp[:, xp.shape[1] - (width - 1):]


def t5_bucket(rel):
    half = NUM_BUCKETS // 2
    max_exact = half // 2
    side = jnp.where(rel > 0, half, 0)
    n = jnp.abs(rel)
    nf = jnp.maximum(n, 1).astype(jnp.float32)
    large = max_exact + (jnp.log(nf / max_exact) / math.log(MAX_DISTANCE / max_exact)
                         * (half - max_exact)).astype(jnp.int32)
    large = jnp.minimum(large, half - 1)
    return side + jnp.where(n < max_exact, n, large)


def dsa_attend(q, qi, wi, q_pos, k, v, kidx, k_pos, rel_bias, top_k):
    b, qb = q.shape[0], q.shape[1]
    s = jnp.einsum('bqhd,bld->bqhl', qi, kidx) * (IDX_DIM ** -0.5)
    score = jnp.einsum('bqh,bqhl->bql', wi, jax.nn.relu(s)).astype(jnp.float32)
    admissible = (k_pos[None, :] // CHUNK) <= (q_pos[:, None] // CHUNK)
    score = jnp.where(admissible[None], score, -jnp.inf)
    top_vals, top_ids = lax.top_k(score, top_k)
    valid = top_vals > -jnp.inf
    gather = jax.vmap(lambda a, i: a[i])
    k_sel = gather(k, top_ids)
    v_sel = gather(v, top_ids)
    rel = k_pos[top_ids] - q_pos[None, :, None]
    bias = rel_bias[t5_bucket(rel)]
    bias = bias.reshape(b, qb, top_k, N_KV_HEADS, GQA_GROUP).transpose(0, 1, 3, 4, 2)
    qg = q.reshape(b, qb, N_KV_HEADS, GQA_GROUP, HEAD_DIM)
    logits = (jnp.einsum('bqngd,bqsnd->bqngs', qg, k_sel).astype(jnp.float32) * (HEAD_DIM ** -0.5)
              + bias.astype(jnp.float32))
    logits = jnp.where(valid[:, :, None, None, :], logits, -jnp.inf)
    p = jax.nn.softmax(logits, axis=-1).astype(v.dtype)
    o = jnp.einsum('bqngs,bqsnd->bqngd', p, v_sel)
    return o.reshape(b, qb, ATTN_WIDTH)


def rglru(x, h0, wa, ba, wx, bx, lam):
    b, t, r = x.shape
    xb = x.reshape(b, t, RNN_BLOCKS, RNN_BLOCK_DIM)
    rg = jax.nn.sigmoid(jnp.einsum('btni,nij->btnj', xb, wa).reshape(b, t, r) + ba)
    ig = jax.nn.sigmoid(jnp.einsum('btni,nij->btnj', xb, wx).reshape(b, t, r) + bx)
    log_a = (-RGLRU_C * rg.astype(jnp.float32)) * jax.nn.softplus(-lam.astype(jnp.float32))
    a = jnp.exp(log_a)
    u = jnp.sqrt(-jnp.expm1(2.0 * log_a)) * (ig * x).astype(jnp.float32)
    u = u.at[:, 0].add(a[:, 0] * h0.astype(jnp.float32))

    def combine(c1, c2):
        a1, b1 = c1
        a2, b2 = c2
        return a1 * a2, a2 * b1 + b2

    _, hs = lax.associative_scan(combine, (a, u), axis=1)
    return hs.astype(x.dtype), hs[:, -1].astype(x.dtype)


def layer(x, ck, cv, ckidx, h0, rconv0, fconv0, norm_mix, w_in, rconv_w, rconv_b,
          wa, ba, wx, bx, lam, rel_bias, w_out, norm_ffn, w_up, w_gate, fconv_w, fconv_b, w_down):
    b, t, _ = x.shape
    h = rmsnorm(x, norm_mix)
    z = h @ w_in
    sizes = [Q_COLS, KV_COLS, KV_COLS, QI_COLS, KI_COLS, WI_COLS, RNN_WIDTH, RNN_WIDTH]
    offsets = np.cumsum(sizes)[:-1].tolist()
    q, k, v, qi, ki, wi, xr, gr = jnp.split(z, offsets, axis=-1)
    q = q.reshape(b, t, N_HEADS, HEAD_DIM)
    k = k.reshape(b, t, N_KV_HEADS, HEAD_DIM)
    v = v.reshape(b, t, N_KV_HEADS, HEAD_DIM)
    qi = qi.reshape(b, t, N_IDX_HEADS, IDX_DIM)
    wi = wi * (N_IDX_HEADS ** -0.5)
    keys_k = jnp.concatenate([ck, k], axis=1)
    keys_v = jnp.concatenate([cv, v], axis=1)
    keys_i = jnp.concatenate([ckidx, ki], axis=1)
    n_keys = keys_k.shape[1]
    past = ck.shape[1]
    k_pos = jnp.arange(n_keys, dtype=jnp.int32)
    q_pos = past + jnp.arange(t, dtype=jnp.int32)
    top_k = min(TOPK_MAX, n_keys // 4)
    if t <= QBLOCK:
        attn = dsa_attend(q, qi, wi, q_pos, keys_k, keys_v, keys_i, k_pos, rel_bias, top_k)
    else:
        nb = t // QBLOCK

        def blk(a):
            return a.reshape((b, nb, QBLOCK) + a.shape[2:]).swapaxes(0, 1)

        def body(args):
            qb_, qib_, wib_, pb_ = args
            return dsa_attend(qb_, qib_, wib_, pb_, keys_k, keys_v, keys_i, k_pos, rel_bias, top_k)

        attn = lax.map(body, (blk(q), blk(qi), blk(wi), q_pos.reshape(nb, QBLOCK)))
        attn = attn.swapaxes(0, 1).reshape(b, t, ATTN_WIDTH)
    xc, rconv_new = causal_dwconv(xr, rconv0, rconv_w, rconv_b)
    hr, h_last = rglru(xc, h0, wa, ba, wx, bx, lam)
    rnn = hr * jax.nn.gelu(gr)
    x = x + jnp.concatenate([attn, rnn], axis=-1) @ w_out
    f = rmsnorm(x, norm_ffn)
    up, fconv_new = causal_dwconv(f @ w_up, fconv0, fconv_w, fconv_b)
    x = x + (jax.nn.gelu(up) * (f @ w_gate)) @ w_down
    return x, (k, v, ki, h_last, rconv_new, fconv_new)


def setup_inputs(seed: int = 0) -> dict:
    key = jax.random.key(seed)
    ks = jax.random.split(key, 32)
    f32 = jnp.float32

    def nrm(k_, shape, scale):
        return jax.random.normal(k_, shape, f32) * scale

    a0 = jax.random.uniform(ks[20], (DEPTH, RNN_WIDTH), f32, 0.9, 0.999)
    s0 = a0 ** (1.0 / RGLRU_C)
    return {
        'x_prompt': nrm(ks[0], (BATCH, SEQ, D_MODEL), 1.0),
        'x_sample': nrm(ks[1], (DEC_BATCH, DEC_SEQ, D_MODEL), 1.0),
        'cache_k': nrm(ks[2], (DEPTH, DEC_BATCH, PAST_LEN, N_KV_HEADS, HEAD_DIM), 1.0),
        'cache_v': nrm(ks[3], (DEPTH, DEC_BATCH, PAST_LEN, N_KV_HEADS, HEAD_DIM), 1.0),
        'cache_kidx': nrm(ks[4], (DEPTH, DEC_BATCH, PAST_LEN, IDX_DIM), 1.0),
        'state_rglru_h': nrm(ks[5], (DEPTH, DEC_BATCH, RNN_WIDTH), 0.5),
        'state_rglru_conv': nrm(ks[6], (DEPTH, DEC_BATCH, RNN_CONV - 1, RNN_WIDTH), 1.0),
        'state_ffn_conv': nrm(ks[7], (DEPTH, DEC_BATCH, FFN_CONV - 1, D_FF), 1.0),
        'norm_mix': 1.0 + nrm(ks[8], (DEPTH, D_MODEL), 0.01),
        'w_in': nrm(ks[9], (DEPTH, D_MODEL, IN_COLS), D_MODEL ** -0.5),
        'rglru_conv_w': nrm(ks[10], (DEPTH, RNN_CONV, RNN_WIDTH), RNN_CONV ** -0.5),
        'rglru_conv_b': nrm(ks[11], (DEPTH, RNN_WIDTH), 0.01),
        'rglru_wa': nrm(ks[12], (DEPTH, RNN_BLOCKS, RNN_BLOCK_DIM, RNN_BLOCK_DIM), RNN_BLOCK_DIM ** -0.5),
        'rglru_ba': nrm(ks[13], (DEPTH, RNN_WIDTH), 0.01),
        'rglru_wx': nrm(ks[14], (DEPTH, RNN_BLOCKS, RNN_BLOCK_DIM, RNN_BLOCK_DIM), RNN_BLOCK_DIM ** -0.5),
        'rglru_bx': nrm(ks[15], (DEPTH, RNN_WIDTH), 0.01),
        'rglru_lambda': jnp.log(s0) - jnp.log1p(-s0),
        'rel_bias': nrm(ks[16], (NUM_BUCKETS, N_HEADS), 0.5),
        'w_out': nrm(ks[17], (DEPTH, MIX_WIDTH, D_MODEL), MIX_WIDTH ** -0.5),
        'norm_ffn': 1.0 + nrm(ks[18], (DEPTH, D_MODEL), 0.01),
        'w_ffn_up': nrm(ks[19], (DEPTH, D_MODEL, D_FF), D_MODEL ** -0.5),
        'w_ffn_gate': nrm(ks[21], (DEPTH, D_MODEL, D_FF), D_MODEL ** -0.5),
        'ffn_conv_w': nrm(ks[22], (DEPTH, FFN_CONV, D_FF), FFN_CONV ** -0.5),
        'ffn_conv_b': nrm(ks[23], (DEPTH, D_FF), 0.01),
        'w_ffn_down': nrm(ks[24], (DEPTH, D_FF, D_MODEL), D_FF ** -0.5),
        'norm_final': 1.0 + nrm(ks[25], (D_MODEL,), 0.01),
    }


def reference(x_prompt, x_sample, cache_k, cache_v, cache_kidx, state_rglru_h, state_rglru_conv,
              state_ffn_conv, norm_mix, w_in, rglru_conv_w, rglru_conv_b, rglru_wa, rglru_ba,
              rglru_wx, rglru_bx, rglru_lambda, rel_bias, w_out, norm_ffn, w_ffn_up, w_ffn_gate,
              ffn_conv_w, ffn_conv_b, w_ffn_down, norm_final):
    bp = x_prompt.shape[0]
    dt = x_prompt.dtype
    xp = x_prompt
    xs = x_sample
    outs_p = []
    outs_s = []
    for i in range(DEPTH):
        lw = (norm_mix[i], w_in[i], rglru_conv_w[i], rglru_conv_b[i], rglru_wa[i], rglru_ba[i],
              rglru_wx[i], rglru_bx[i], rglru_lambda[i], rel_bias, w_out[i], norm_ffn[i],
              w_ffn_up[i], w_ffn_gate[i], ffn_conv_w[i], ffn_conv_b[i], w_ffn_down[i])
        xp, st_p = layer(xp,
                         jnp.zeros((bp, 0, N_KV_HEADS, HEAD_DIM), dt),
                         jnp.zeros((bp, 0, N_KV_HEADS, HEAD_DIM), dt),
                         jnp.zeros((bp, 0, IDX_DIM), dt),
                         jnp.zeros((bp, RNN_WIDTH), dt),
                         jnp.zeros((bp, RNN_CONV - 1, RNN_WIDTH), dt),
                         jnp.zeros((bp, FFN_CONV - 1, D_FF), dt),
                         *lw)
        xs, st_s = layer(xs, cache_k[i], cache_v[i], cache_kidx[i], state_rglru_h[i],
                         state_rglru_conv[i], state_ffn_conv[i], *lw)
        outs_p.append(st_p)
        outs_s.append(st_s)
    y_prompt = rmsnorm(xp, norm_final)
    y_sample = rmsnorm(xs, norm_final)
    k_p = jnp.stack([o[0] for o in outs_p])
    v_p = jnp.stack([o[1] for o in outs_p])
    kidx_p = jnp.stack([o[2] for o in outs_p])
    h_p = jnp.stack([o[3] for o in outs_p])
    rconv_p = jnp.stack([o[4] for o in outs_p])
    fconv_p = jnp.stack([o[5] for o in outs_p])
    k_s = jnp.stack([o[0] for o in outs_s])
    v_s = jnp.stack([o[1] for o in outs_s])
    kidx_s = jnp.stack([o[2] for o in outs_s])
    h_s = jnp.stack([o[3] for o in outs_s])
    rconv_s = jnp.stack([o[4] for o in outs_s])
    fconv_s = jnp.stack([o[5] for o in outs_s])
    return (y_prompt, y_sample, k_p, v_p, kidx_p, h_p, rconv_p, fconv_p,
            k_s, v_s, kidx_s, h_s, rconv_s, fconv_s)
```

```python
import functools
import math

import jax
import jax.numpy as jnp
import numpy as np
from jax import lax
from jax.experimental import pallas as pl
from jax.experimental.pallas import tpu as pltpu

F32 = jnp.float32
BF16 = jnp.bfloat16
I32 = jnp.int32

CHUNK = 64
N_HEADS = 8
HEAD_DIM = 64
N_KV_HEADS = 2
GQA_GROUP = N_HEADS // N_KV_HEADS
ATTN_WIDTH = N_HEADS * HEAD_DIM
N_IDX_HEADS = 8
IDX_DIM = 64
TOPK_MAX = 256
NUM_BUCKETS = 32
MAX_DISTANCE = 128
RNN_BLOCKS = 8
RNN_CONV = 4
RGLRU_C = 8.0
FFN_CONV = 3
EPS = 1e-6

LANES = 128
SUBLANES = 8
VMEM_LIMIT_BYTES = 56 * 1024 * 1024

Q_TILE = 128
KEY_TILE = 256
MASKED = -1e30
INT_MIN = -(2 ** 31)
INT_MAX = 2 ** 31 - 1

_C_Q = 0
_C_K = _C_Q + ATTN_WIDTH
_C_V = _C_K + LANES
_C_QI = _C_V + LANES
_C_KW = _C_QI + N_IDX_HEADS * IDX_DIM
_C_KK = _C_KW + LANES
_C_XR = _C_KK + LANES


def _rmsnorm(x, g):
    y = x * lax.rsqrt(jnp.mean(x * x, axis=-1, keepdims=True) + EPS)
    return y * g


def _proj_kernel(x_ref, g_ref, w_ref, q_ref, k_ref, v_ref, kb_ref, vb_ref, qi_ref, ki_ref,
                 kk_ref, kw_ref, xr_ref, gr_ref, *, rnn_width):
    h = _rmsnorm(x_ref[...], g_ref[...]).astype(BF16)

    def proj(start, width):
        return jnp.dot(h, w_ref[:, start:start + width], preferred_element_type=F32)

    q_ref[...] = (proj(_C_Q, ATTN_WIDTH) * (HEAD_DIM ** -0.5)).astype(BF16)
    k = proj(_C_K, LANES)
    v = proj(_C_V, LANES)
    k_ref[...] = k
    v_ref[...] = v
    kb_ref[...] = k.astype(BF16)
    vb_ref[...] = v.astype(BF16)
    qi_ref[...] = proj(_C_QI, N_IDX_HEADS * IDX_DIM).astype(BF16)
    kw = proj(_C_KW, LANES)
    kw_ref[...] = kw
    ki_ref[...] = kw[:, :IDX_DIM]
    kk_ref[...] = proj(_C_KK, LANES).astype(BF16)
    xr_ref[...] = proj(_C_XR, rnn_width)
    gr_ref[...] = proj(_C_XR + rnn_width, rnn_width)


def _project(x2d, g, w_packed, rnn_width, tm):
    n, d = x2d.shape
    ncol = w_packed.shape[1]
    row = lambda w: pl.BlockSpec((tm, w), lambda i: (i, 0))
    out_shape = (
        jax.ShapeDtypeStruct((n, ATTN_WIDTH), BF16),
        jax.ShapeDtypeStruct((n, LANES), F32),
        jax.ShapeDtypeStruct((n, LANES), F32),
        jax.ShapeDtypeStruct((n, LANES), BF16),
        jax.ShapeDtypeStruct((n, LANES), BF16),
        jax.ShapeDtypeStruct((n, N_IDX_HEADS * IDX_DIM), BF16),
        jax.ShapeDtypeStruct((n, IDX_DIM), F32),
        jax.ShapeDtypeStruct((n, LANES), BF16),
        jax.ShapeDtypeStruct((n, LANES), F32),
        jax.ShapeDtypeStruct((n, rnn_width), F32),
        jax.ShapeDtypeStruct((n, rnn_width), F32),
    )
    out_specs = (row(ATTN_WIDTH), row(LANES), row(LANES), row(LANES), row(LANES),
                 row(N_IDX_HEADS * IDX_DIM), row(IDX_DIM), row(LANES), row(LANES),
                 row(rnn_width), row(rnn_width))
    return pl.pallas_call(
        functools.partial(_proj_kernel, rnn_width=rnn_width),
        grid=(n // tm,),
        in_specs=[row(d), pl.BlockSpec((1, d), lambda i: (0, 0)),
                  pl.BlockSpec((d, ncol), lambda i: (0, 0))],
        out_specs=out_specs,
        out_shape=out_shape,
        compiler_params=pltpu.CompilerParams(dimension_semantics=("arbitrary",),
                                             vmem_limit_bytes=VMEM_LIMIT_BYTES),
        name="proj",
    )(x2d, g, w_packed)


def _bucket_tables():
    half = NUM_BUCKETS // 2
    max_exact = half // 2
    r = np.arange(Q_TILE, dtype=np.int64)[:, None]
    c = np.arange(KEY_TILE, dtype=np.int64)[None, :]
    tabs = []
    for i in range(3):
        rel = (i - 2) * LANES + c - r
        side = np.where(rel > 0, half, 0)
        n = np.abs(rel)
        nf = np.maximum(n, 1).astype(np.float32)
        ratio = np.log(nf / np.float32(max_exact)) / np.float32(math.log(MAX_DISTANCE / max_exact))
        large = max_exact + (ratio * np.float32(half - max_exact)).astype(np.int32)
        large = np.minimum(large, half - 1)
        tabs.append(side + np.where(n < max_exact, n, large))
    return np.stack(tabs).astype(np.int32)


def _key_to_float(k):
    bits = jnp.where(k < 0, (-k) | INT_MIN, k)
    return lax.bitcast_convert_type(bits, F32)


def _attn_kernel(rb_ref, bkt_ref, q_ref, qi_ref, kw_ref, kk_ref, kb_ref, vb_ref, o_ref,
                 sc_ref, qim_ref, qm_ref, tbl_ref, m_ref, l_ref, acc_ref, tie_ref,
                 *, past, n_keys, top_k):
    tq, lt = Q_TILE, KEY_TILE
    far_bucket = NUM_BUCKETS // 2 - 1
    j = pl.program_id(1)
    q0 = past + j * tq
    lane = lax.broadcasted_iota(I32, (tq, LANES), 1)
    lo_half = lane < HEAD_DIM

    @pl.when((pl.program_id(0) == 0) & (j == 0))
    def _():
        for i in range(3):
            bkt = bkt_ref[i]
            for slot, h in enumerate(_HEAD_ORDER):
                t = jnp.zeros((tq, lt), F32)
                for b in range(NUM_BUCKETS):
                    t = jnp.where(bkt == b, rb_ref[b, h], t)
                tbl_ref[i, slot] = t - rb_ref[far_bucket, h]

    for h in range(N_HEADS):
        c = h // 2
        keep = lo_half if h % 2 == 0 else jnp.logical_not(lo_half)
        qi_c = qi_ref[:, c * LANES:(c + 1) * LANES]
        qim_ref[h * tq:(h + 1) * tq, :] = jnp.where(keep, qi_c, jnp.zeros_like(qi_c))
        q_c = q_ref[:, c * LANES:(c + 1) * LANES]
        qm_ref[h * tq:(h + 1) * tq, :] = jnp.where(keep, q_c, jnp.zeros_like(q_c))

    wi_t = kw_ref[...].T[IDX_DIM:IDX_DIM + N_IDX_HEADS, :]
    wi_t = (wi_t * (N_IDX_HEADS ** -0.5)) * (IDX_DIM ** -0.5)

    qpos = q0 + lax.broadcasted_iota(I32, (1, tq), 1)
    n_adm = jnp.minimum(((qpos // CHUNK) + 1) * CHUNK, n_keys)
    n_adm_max = jnp.minimum(((q0 + tq - 1) // CHUNK + 1) * CHUNK, n_keys)
    n_kt = (n_adm_max + lt - 1) // lt
    row = lax.broadcasted_iota(I32, (lt, tq), 0)

    def score_body(kt, carry):
        base = pl.multiple_of(kt * lt, lt)
        s = lax.dot_general(kk_ref[pl.ds(base, lt), :], qim_ref[...],
                            (((1,), (1,)), ((), ())), preferred_element_type=F32)
        acc = jnp.zeros((lt, tq), F32)
        for h in range(N_IDX_HEADS):
            acc = acc + wi_t[h:h + 1, :] * jnp.maximum(s[:, h * tq:(h + 1) * tq], 0.0)
        sc_ref[pl.ds(base, lt), :] = jnp.where(row + base < n_adm, acc, -jnp.inf)
        return carry

    lax.fori_loop(0, n_kt, score_body, 0)

    def count(pred):
        def body(kt, cnt):
            base = pl.multiple_of(kt * lt, lt)
            hit = pred(sc_ref[pl.ds(base, lt), :], base).astype(I32)
            return cnt + jnp.sum(hit.reshape(lt // SUBLANES, SUBLANES, tq), axis=0)
        cnt = lax.fori_loop(0, n_kt, body, jnp.zeros((SUBLANES, tq), I32))
        return jnp.sum(cnt, axis=0, keepdims=True)

    def bisect(i, t):
        cand = t + (jnp.int32(1) << (31 - i))
        cand_f = _key_to_float(cand)
        return jnp.where(count(lambda blk, base: blk >= cand_f) >= top_k, cand, t)

    t_key = lax.fori_loop(0, 32, bisect, jnp.full((1, tq), INT_MIN, I32))
    thr = _key_to_float(t_key)
    n_gt = count(lambda blk, base: blk > thr)
    n_eq = count(lambda blk, base: blk == thr)
    want = top_k - n_gt
    finite_thr = thr > -jnp.inf
    need = (n_eq > want) & finite_thr
    tie_ref[...] = jnp.where(finite_thr, INT_MAX, -1)

    @pl.when(jnp.max(need.astype(I32)) > 0)
    def _():
        nbits = max(1, int(sc_ref.shape[0]).bit_length())

        def tie_step(i, m):
            cand = m | (jnp.int32(1) << (nbits - 1 - i))
            c = count(lambda blk, base: (blk == thr) & (row + base < cand))
            return jnp.where(c < want, cand, m)

        m = lax.fori_loop(0, nbits, tie_step, jnp.zeros((1, tq), I32))
        tie_ref[...] = jnp.where(need, m, tie_ref[...])

    tie_max = tie_ref[...]

    m_ref[...] = jnp.full(m_ref.shape, MASKED, F32)
    l_ref[...] = jnp.zeros(l_ref.shape, F32)
    acc_ref[...] = jnp.zeros(acc_ref.shape, F32)
    n_far = jnp.clip((q0 - (LANES - 1)) // lt, 0, n_kt)

    def attn_body(kt, near):
        base = pl.multiple_of(kt * lt, lt)
        blk = sc_ref[pl.ds(base, lt), :]
        sel = (blk > thr) | ((blk == thr) & (row + base <= tie_max))
        mask_t = jnp.where(sel, 0.0, MASKED).T
        s = lax.dot_general(qm_ref[...], kb_ref[pl.ds(base, lt), :],
                            (((1,), (1,)), ((), ())), preferred_element_type=F32)
        s = s.reshape(N_HEADS, tq, lt) + mask_t[None]
        if near:
            s = s + tbl_ref[(base - q0) // LANES + 2]
        s = s.reshape(N_HEADS * tq, lt)
        m_old = m_ref[...]
        m_new = jnp.maximum(m_old, jnp.max(s, axis=-1, keepdims=True))
        alpha = jnp.exp(m_old - m_new)
        p = jnp.exp(s - m_new)
        l_ref[...] = alpha * l_ref[...] + jnp.sum(p, axis=-1, keepdims=True)
        pv = jnp.dot(p.astype(BF16), vb_ref[pl.ds(base, lt), :], preferred_element_type=F32)
        acc_ref[...] = alpha * acc_ref[...] + pv
        m_ref[...] = m_new

    def far_body(kt, carry):
        attn_body(kt, False)
        return carry

    def near_body(kt, carry):
        attn_body(kt, True)
        return carry

    lax.fori_loop(0, n_far, far_body, 0)
    lax.fori_loop(n_far, n_kt, near_body, 0)

    out = acc_ref[...] / l_ref[...]
    for c in range(N_HEADS // 2):
        lo = out[(2 * c) * tq:(2 * c + 1) * tq, :]
        hi = out[(2 * c + 1) * tq:(2 * c + 2) * tq, :]
        o_ref[:, c * LANES:(c + 1) * LANES] = jnp.where(lo_half, lo, hi).astype(o_ref.dtype)


def _attention(rel_bias, q, qi, kw, kk, kb, vb, *, past, n_keys):
    b, tq_total, _ = q.shape
    lp = kk.shape[1]
    top_k = min(TOPK_MAX, n_keys // 4)
    assert tq_total % Q_TILE == 0 and lp % KEY_TILE == 0 and past % KEY_TILE == 0
    assert lp >= n_keys and top_k <= KEY_TILE
    bkt = jnp.asarray(_bucket_tables())
    qspec = lambda w: pl.BlockSpec((None, Q_TILE, w), lambda bi, j: (bi, j, 0))
    kspec = pl.BlockSpec((None, lp, LANES), lambda bi, j: (bi, 0, 0))
    return pl.pallas_call(
        functools.partial(_attn_kernel, past=past, n_keys=n_keys, top_k=top_k),
        grid=(b, tq_total // Q_TILE),
        in_specs=[pl.BlockSpec(memory_space=pltpu.SMEM),
                  pl.BlockSpec((3, Q_TILE, KEY_TILE), lambda bi, j: (0, 0, 0)),
                  qspec(ATTN_WIDTH), qspec(N_IDX_HEADS * IDX_DIM), qspec(LANES),
                  kspec, kspec, kspec],
        out_specs=qspec(ATTN_WIDTH),
        out_shape=jax.ShapeDtypeStruct((b, tq_total, ATTN_WIDTH), BF16),
        scratch_shapes=[
            pltpu.VMEM((lp, Q_TILE), F32),
            pltpu.VMEM((N_IDX_HEADS * Q_TILE, LANES), BF16),
            pltpu.VMEM((N_HEADS * Q_TILE, LANES), BF16),
            pltpu.VMEM((3, N_HEADS, Q_TILE, KEY_TILE), F32),
            pltpu.VMEM((N_HEADS * Q_TILE, 1), F32),
            pltpu.VMEM((N_HEADS * Q_TILE, 1), F32),
            pltpu.VMEM((N_HEADS * Q_TILE, LANES), F32),
            pltpu.VMEM((1, Q_TILE), I32),
        ],
        compiler_params=pltpu.CompilerParams(dimension_semantics=("arbitrary", "arbitrary"),
                                             vmem_limit_bytes=VMEM_LIMIT_BYTES),
        name="attn",
    )(rel_bias, bkt, q, qi, kw, kk, kb, vb)


def _softplus(x):
    return jnp.maximum(x, 0.0) + jnp.log1p(jnp.exp(-jnp.abs(x)))


def _rglru_kernel(xr_ref, gr_ref, h0_ref, c0_ref, cw_ref, cb_ref, wa_ref, ba_ref, wx_ref, bx_ref,
                  lam_ref, rnn_ref, hl_ref, xs_ref, a_ref, u_ref, hs_ref, hc_ref, *, tm):
    t = pl.program_id(1)
    pad = SUBLANES
    hist = RNN_CONV - 1

    @pl.when(t == 0)
    def _():
        xs_ref[pad - hist:pad, :] = c0_ref[...]
        hc_ref[...] = jnp.broadcast_to(h0_ref[...], hc_ref.shape)

    xs_ref[pad:pad + tm, :] = xr_ref[...]
    xc = cb_ref[...] + cw_ref[0:1, :] * xs_ref[pad - hist:pad - hist + tm, :]
    for jj in range(1, RNN_CONV):
        xc = xc + cw_ref[jj:jj + 1, :] * xs_ref[pad - hist + jj:pad - hist + jj + tm, :]
    xs_ref[pad - hist:pad, :] = xs_ref[pad + tm - hist:pad + tm, :]

    xcb = xc.astype(BF16)
    rg = jax.nn.sigmoid(jnp.dot(xcb, wa_ref[...], preferred_element_type=F32) + ba_ref[...])
    ig = jax.nn.sigmoid(jnp.dot(xcb, wx_ref[...], preferred_element_type=F32) + bx_ref[...])
    log_a = (-RGLRU_C * rg) * _softplus(-lam_ref[...])
    a = jnp.exp(log_a)
    u = jnp.sqrt(-jnp.tanh(log_a) * (jnp.exp(2.0 * log_a) + 1.0)) * (ig * xc)

    row = lax.broadcasted_iota(I32, a.shape, 0) % SUBLANES
    for s in (1, 2, 4):
        a_prev = pltpu.roll(a, s, 0)
        u_prev = pltpu.roll(u, s, 0)
        use = row >= s
        u = jnp.where(use, a * u_prev + u, u)
        a = jnp.where(use, a * a_prev, a)
    a_ref[...] = a
    u_ref[...] = u

    def carry_body(g, h):
        base = pl.multiple_of(g * SUBLANES, SUBLANES)
        hg = a_ref[pl.ds(base, SUBLANES), :] * h + u_ref[pl.ds(base, SUBLANES), :]
        hs_ref[pl.ds(base, SUBLANES), :] = hg
        return jnp.broadcast_to(hg[SUBLANES - 1:SUBLANES, :], hg.shape)

    h_end = lax.fori_loop(0, tm // SUBLANES, carry_body, hc_ref[...])
    hc_ref[...] = h_end
    rnn_ref[...] = (hs_ref[...] * jax.nn.gelu(gr_ref[...])).astype(rnn_ref.dtype)

    @pl.when(t == pl.num_programs(1) - 1)
    def _():
        hl_ref[...] = h_end[0:1, :]


def _rglru(xr, gr, h0, conv0, conv_w, conv_b, wa_bd, ba, wx_bd, bx, lam, tm):
    b, t, r = xr.shape
    assert t % tm == 0 and tm % SUBLANES == 0
    tile = pl.BlockSpec((None, tm, r), lambda bi, ti: (bi, ti, 0))
    per_b = lambda rows: pl.BlockSpec((None, rows, r), lambda bi, ti: (bi, 0, 0))
    full = lambda a: pl.BlockSpec(a.shape, lambda bi, ti: (0,) * a.ndim)
    return pl.pallas_call(
        functools.partial(_rglru_kernel, tm=tm),
        grid=(b, t // tm),
        in_specs=[tile, tile, per_b(1), per_b(RNN_CONV - 1), full(conv_w), full(conv_b),
                  full(wa_bd), full(ba), full(wx_bd), full(bx), full(lam)],
        out_specs=(tile, per_b(1)),
        out_shape=(jax.ShapeDtypeStruct((b, t, r), BF16), jax.ShapeDtypeStruct((b, 1, r), F32)),
        scratch_shapes=[pltpu.VMEM((tm + SUBLANES, r), F32), pltpu.VMEM((tm, r), F32),
                        pltpu.VMEM((tm, r), F32), pltpu.VMEM((tm, r), F32),
                        pltpu.VMEM((SUBLANES, r), F32)],
        compiler_params=pltpu.CompilerParams(dimension_semantics=("arbitrary", "arbitrary"),
                                             vmem_limit_bytes=VMEM_LIMIT_BYTES),
        name="rglru",
    )(xr, gr, h0, conv0, conv_w, conv_b, wa_bd, ba, wx_bd, bx, lam)


def _ffn_kernel(x_ref, at_ref, rn_ref, c0_ref, woa_ref, wor_ref, gf_ref, wu_ref, wg_ref, cw_ref,
                cb_ref, wd_ref, gl_ref, y_ref, fc_ref, up_ref, *, tm, final_norm):
    t = pl.program_id(1)
    pad = SUBLANES
    hist = FFN_CONV - 1

    @pl.when(t == 0)
    def _():
        up_ref[pad - hist:pad, :] = c0_ref[...]

    x1 = (x_ref[...]
          + jnp.dot(at_ref[...], woa_ref[...], preferred_element_type=F32)
          + jnp.dot(rn_ref[...], wor_ref[...], preferred_element_type=F32))
    f = _rmsnorm(x1, gf_ref[...]).astype(BF16)
    up_ref[pad:pad + tm, :] = jnp.dot(f, wu_ref[...], preferred_element_type=F32)
    up = cb_ref[...] + cw_ref[0:1, :] * up_ref[pad - hist:pad - hist + tm, :]
    for jj in range(1, FFN_CONV):
        up = up + cw_ref[jj:jj + 1, :] * up_ref[pad - hist + jj:pad - hist + jj + tm, :]
    tail = up_ref[pad + tm - hist:pad + tm, :]
    up_ref[pad - hist:pad, :] = tail
    gate = jnp.dot(f, wg_ref[...], preferred_element_type=F32)
    act = (jax.nn.gelu(up) * gate).astype(BF16)
    x2 = x1 + jnp.dot(act, wd_ref[...], preferred_element_type=F32)
    y_ref[...] = _rmsnorm(x2, gl_ref[...]) if final_norm else x2

    @pl.when(t == pl.num_programs(1) - 1)
    def _():
        fc_ref[...] = tail


def _ffn(x, attn, rnn, conv0, wo_a, wo_r, g_ffn, w_up, w_gate, conv_w, conv_b, w_down, g_last,
         tm, final_norm):
    b, t, d = x.shape
    dff = w_up.shape[1]
    assert t % tm == 0
    tile = lambda w: pl.BlockSpec((None, tm, w), lambda bi, ti: (bi, ti, 0))
    per_b = pl.BlockSpec((None, FFN_CONV - 1, dff), lambda bi, ti: (bi, 0, 0))
    full = lambda a: pl.BlockSpec(a.shape, lambda bi, ti: (0,) * a.ndim,
                                  pipeline_mode=pl.Buffered(1))
    return pl.pallas_call(
        functools.partial(_ffn_kernel, tm=tm, final_norm=final_norm),
        grid=(b, t // tm),
        in_specs=[tile(d), tile(attn.shape[-1]), tile(rnn.shape[-1]), per_b, full(wo_a), full(wo_r),
                  full(g_ffn), full(w_up), full(w_gate), full(conv_w), full(conv_b), full(w_down),
                  full(g_last)],
        out_specs=(tile(d), per_b),
        out_shape=(jax.ShapeDtypeStruct((b, t, d), F32),
                   jax.ShapeDtypeStruct((b, FFN_CONV - 1, dff), F32)),
        scratch_shapes=[pltpu.VMEM((tm + SUBLANES, dff), F32)],
        compiler_params=pltpu.CompilerParams(dimension_semantics=("arbitrary", "arbitrary"),
                                             vmem_limit_bytes=VMEM_LIMIT_BYTES),
        name="ffn",
    )(x, attn, rnn, conv0, wo_a, wo_r, g_ffn, w_up, w_gate, conv_w, conv_b, w_down, g_last)


_HEAD_ORDER = tuple(h for c in range(GQA_GROUP) for h in (c, GQA_GROUP + c))


def _pack_layer_weights(w_in, wa, wx, w_out, w_up, w_gate, w_down):
    d = w_in.shape[0]
    sizes = [ATTN_WIDTH, N_KV_HEADS * HEAD_DIM, N_KV_HEADS * HEAD_DIM, N_IDX_HEADS * IDX_DIM,
             IDX_DIM, N_IDX_HEADS]
    offs = np.cumsum([0] + sizes)
    wq, wk, wv, wqi, wki, wwi = (w_in[:, offs[i]:offs[i + 1]] for i in range(6))
    wrest = w_in[:, offs[6]:]
    rnn_width = wrest.shape[1] // 2
    order = np.asarray(_HEAD_ORDER)
    wq = wq.reshape(d, N_HEADS, HEAD_DIM)[:, order, :].reshape(d, ATTN_WIDTH)
    zpad = jnp.zeros((d, LANES - IDX_DIM - N_IDX_HEADS), w_in.dtype)
    packed = jnp.concatenate([wq, wk, wv, wqi, wki, wwi, zpad, wki, wki, wrest], axis=1).astype(BF16)
    eye = jnp.eye(RNN_BLOCKS, dtype=wa.dtype)
    bd = lambda w: jnp.einsum("nij,nm->nimj", w, eye).reshape(rnn_width, rnn_width).astype(BF16)
    wo_a = w_out[:ATTN_WIDTH].reshape(N_HEADS, HEAD_DIM, -1)[order].reshape(ATTN_WIDTH, -1)
    wo_r = w_out[ATTN_WIDTH:]
    return dict(w_packed=packed, rnn_width=rnn_width, wa_bd=bd(wa), wx_bd=bd(wx),
                wo_a=wo_a.astype(BF16), wo_r=wo_r.astype(BF16), w_up=w_up.astype(BF16),
                w_gate=w_gate.astype(BF16), w_down=w_down.astype(BF16))


def _pad_rows(a, rows):
    return a if a.shape[1] == rows else jnp.pad(a, ((0, 0), (0, rows - a.shape[1]), (0, 0)))


def _round_up(n, m):
    return -(-n // m) * m


def _layer(x, ck, cv, ckidx, h0, rconv0, fconv0, pw, norm_mix, conv_w, conv_b, ba, bx, lam,
           rel_bias, norm_ffn, fconv_w, fconv_b, norm_last, final_norm):
    b, t, d = x.shape
    past = ck.shape[1]
    r = pw["rnn_width"]
    n_tok = b * t
    tm = min(512, n_tok)
    (q, k, v, kb, vb, qi, ki, kk, kw, xr, gr) = _project(
        x.reshape(n_tok, d), norm_mix[None, :], pw["w_packed"], r, tm)
    b3 = lambda a: a.reshape(b, t, a.shape[-1])

    n_keys = past + t
    lp = _round_up(n_keys, KEY_TILE)
    tq = _round_up(t, Q_TILE)
    if past:
        ck2 = ck.reshape(b, past, -1).astype(BF16)
        cv2 = cv.reshape(b, past, -1).astype(BF16)
        cki = ckidx.astype(BF16)
        keys_k = jnp.concatenate([ck2, b3(kb)], axis=1)
        keys_v = jnp.concatenate([cv2, b3(vb)], axis=1)
        keys_i = jnp.concatenate([jnp.concatenate([cki, cki], axis=-1), b3(kk)], axis=1)
    else:
        keys_k, keys_v, keys_i = b3(kb), b3(vb), b3(kk)
    attn = _attention(rel_bias, _pad_rows(b3(q), tq), _pad_rows(b3(qi), tq), _pad_rows(b3(kw), tq),
                      _pad_rows(keys_i, lp), _pad_rows(keys_k, lp), _pad_rows(keys_v, lp),
                      past=past, n_keys=n_keys)[:, :t]

    tr = min(512, t)
    rnn, h_last = _rglru(b3(xr), b3(gr), h0[:, None, :], rconv0, conv_w, conv_b[None, :],
                         pw["wa_bd"], ba[None, :], pw["wx_bd"], bx[None, :], lam[None, :], tr)
    assert t >= RNN_CONV - 1 and t >= FFN_CONV - 1
    rconv_new = b3(xr)[:, t - (RNN_CONV - 1):, :]

    tf = min(256, t)
    y, fconv_new = _ffn(x, attn, rnn, fconv0, pw["wo_a"], pw["wo_r"], norm_ffn[None, :], pw["w_up"],
                        pw["w_gate"], fconv_w, fconv_b[None, :], pw["w_down"], norm_last[None, :],
                        tf, final_norm)
    k_new = k.reshape(b, t, N_KV_HEADS, HEAD_DIM)
    v_new = v.reshape(b, t, N_KV_HEADS, HEAD_DIM)
    return y, (k_new, v_new, b3(ki), h_last[:, 0, :], rconv_new, fconv_new)


def kernel(x_prompt, x_sample, cache_k, cache_v, cache_kidx, state_rglru_h, state_rglru_conv,
           state_ffn_conv, norm_mix, w_in, rglru_conv_w, rglru_conv_b, rglru_wa, rglru_ba,
           rglru_wx, rglru_bx, rglru_lambda, rel_bias, w_out, norm_ffn, w_ffn_up, w_ffn_gate,
           ffn_conv_w, ffn_conv_b, w_ffn_down, norm_final):
    depth = w_in.shape[0]
    bp = x_prompt.shape[0]
    dt = x_prompt.dtype
    xp, xs = x_prompt, x_sample
    outs_p, outs_s = [], []
    for i in range(depth):
        pw = _pack_layer_weights(w_in[i], rglru_wa[i], rglru_wx[i], w_out[i], w_ffn_up[i],
                                 w_ffn_gate[i], w_ffn_down[i])
        r = pw["rnn_width"]
        last = i == depth - 1
        lw = (pw, norm_mix[i], rglru_conv_w[i], rglru_conv_b[i], rglru_ba[i], rglru_bx[i],
              rglru_lambda[i], rel_bias, norm_ffn[i], ffn_conv_w[i], ffn_conv_b[i], norm_final, last)
        xp, st_p = _layer(xp,
                          jnp.zeros((bp, 0, N_KV_HEADS, HEAD_DIM), dt),
                          jnp.zeros((bp, 0, N_KV_HEADS, HEAD_DIM), dt),
                          jnp.zeros((bp, 0, IDX_DIM), dt),
                          jnp.zeros((bp, r), dt),
                          jnp.zeros((bp, RNN_CONV - 1, r), dt),
                          jnp.zeros((bp, FFN_CONV - 1, w_ffn_up.shape[-1]), dt),
                          *lw)
        xs, st_s = _layer(xs, cache_k[i], cache_v[i], cache_kidx[i], state_rglru_h[i],
                          state_rglru_conv[i], state_ffn_conv[i], *lw)
        outs_p.append(st_p)
        outs_s.append(st_s)
    stack = lambda outs, j: jnp.stack([o[j] for o in outs])
    return ((xp, xs) + tuple(stack(outs_p, j) for j in range(6))
            + tuple(stack(outs_s, j) for j in range(6)))
```

```python
import functools
import math

import jax
import jax.numpy as jnp
import numpy as np
from jax import lax
from jax.experimental import pallas as pl
from jax.experimental.pallas import tpu as pltpu

F32 = jnp.float32
BF16 = jnp.bfloat16
I32 = jnp.int32

CHUNK = 64
N_HEADS = 8
HEAD_DIM = 64
N_KV_HEADS = 2
GQA_GROUP = N_HEADS // N_KV_HEADS
ATTN_WIDTH = N_HEADS * HEAD_DIM
N_IDX_HEADS = 8
IDX_DIM = 64
TOPK_MAX = 256
NUM_BUCKETS = 32
MAX_DISTANCE = 128
RNN_BLOCKS = 8
RNN_CONV = 4
RGLRU_C = 8.0
FFN_CONV = 3
EPS = 1e-6

LANES = 128
SUBLANES = 8
VMEM_LIMIT_BYTES = 56 * 1024 * 1024

Q_TILE = 128
KEY_TILE = 256
MASKED = -1e30
INT_MIN = -(2 ** 31)
INT_MAX = 2 ** 31 - 1

_C_Q = 0
_C_K = _C_Q + ATTN_WIDTH
_C_V = _C_K + LANES
_C_QI = _C_V + LANES
_C_KW = _C_QI + N_IDX_HEADS * IDX_DIM
_C_KK = _C_KW + LANES
_C_XR = _C_KK + LANES


def _rmsnorm(x, g):
    y = x * lax.rsqrt(jnp.mean(x * x, axis=-1, keepdims=True) + EPS)
    return y * g


def _proj_kernel(x_ref, g_ref, w_ref, q_ref, k_ref, v_ref, kb_ref, vb_ref, qi_ref, ki_ref,
                 kk_ref, kw_ref, xr_ref, gr_ref, *, rnn_width):
    h = _rmsnorm(x_ref[...], g_ref[...]).astype(BF16)

    def proj(start, width):
        return jnp.dot(h, w_ref[:, start:start + width], preferred_element_type=F32)

    q_ref[...] = (proj(_C_Q, ATTN_WIDTH) * (HEAD_DIM ** -0.5)).astype(BF16)
    k = proj(_C_K, LANES)
    v = proj(_C_V, LANES)
    k_ref[...] = k
    v_ref[...] = v
    kb_ref[...] = k.astype(BF16)
    vb_ref[...] = v.astype(BF16)
    qi_ref[...] = proj(_C_QI, N_IDX_HEADS * IDX_DIM).astype(BF16)
    kw = proj(_C_KW, LANES)
    kw_ref[...] = kw
    ki_ref[...] = kw[:, :IDX_DIM]
    kk_ref[...] = proj(_C_KK, LANES).astype(BF16)
    xr_ref[...] = proj(_C_XR, rnn_width)
    gr_ref[...] = proj(_C_XR + rnn_width, rnn_width)


def _project(x2d, g, w_packed, rnn_width, tm):
    n, d = x2d.shape
    ncol = w_packed.shape[1]
    row = lambda w: pl.BlockSpec((tm, w), lambda i: (i, 0))
    out_shape = (
        jax.ShapeDtypeStruct((n, ATTN_WIDTH), BF16),
        jax.ShapeDtypeStruct((n, LANES), F32),
        jax.ShapeDtypeStruct((n, LANES), F32),
        jax.ShapeDtypeStruct((n, LANES), BF16),
        jax.ShapeDtypeStruct((n, LANES), BF16),
        jax.ShapeDtypeStruct((n, N_IDX_HEADS * IDX_DIM), BF16),
        jax.ShapeDtypeStruct((n, IDX_DIM), F32),
        jax.ShapeDtypeStruct((n, LANES), BF16),
        jax.ShapeDtypeStruct((n, LANES), F32),
        jax.ShapeDtypeStruct((n, rnn_width), F32),
        jax.ShapeDtypeStruct((n, rnn_width), F32),
    )
    out_specs = (row(ATTN_WIDTH), row(LANES), row(LANES), row(LANES), row(LANES),
                 row(N_IDX_HEADS * IDX_DIM), row(IDX_DIM), row(LANES), row(LANES),
                 row(rnn_width), row(rnn_width))
    return pl.pallas_call(
        functools.partial(_proj_kernel, rnn_width=rnn_width),
        grid=(n // tm,),
        in_specs=[row(d), pl.BlockSpec((1, d), lambda i: (0, 0)),
                  pl.BlockSpec((d, ncol), lambda i: (0, 0))],
        out_specs=out_specs,
        out_shape=out_shape,
        compiler_params=pltpu.CompilerParams(dimension_semantics=("arbitrary",),
                                             vmem_limit_bytes=VMEM_LIMIT_BYTES),
        name="proj",
    )(x2d, g, w_packed)


def _bucket_tables():
    half = NUM_BUCKETS // 2
    max_exact = half // 2
    r = np.arange(Q_TILE, dtype=np.int64)[None, :]
    c = np.arange(KEY_TILE, dtype=np.int64)[:, None]
    tabs = []
    for i in range(3):
        rel = (i - 2) * LANES + c - r
        side = np.where(rel > 0, half, 0)
        n = np.abs(rel)
        nf = np.maximum(n, 1).astype(np.float32)
        ratio = np.log(nf / np.float32(max_exact)) / np.float32(math.log(MAX_DISTANCE / max_exact))
        large = max_exact + (ratio * np.float32(half - max_exact)).astype(np.int32)
        large = np.minimum(large, half - 1)
        tabs.append(side + np.where(n < max_exact, n, large))
    return np.stack(tabs).astype(np.int32)


def _key_to_float(k):
    bits = jnp.where(k < 0, (-k) | INT_MIN, k)
    return lax.bitcast_convert_type(bits, F32)


def _attn_kernel(rb_ref, bkt_ref, q_ref, qi_ref, kw_ref, kk_ref, kb_ref, vt_ref, o_ref,
                 sc_ref, qit_ref, qt_ref, tbl_ref, m_ref, l_ref, acc_ref, tie_ref,
                 *, past, n_keys, top_k):
    tq, lt = Q_TILE, KEY_TILE
    far_bucket = NUM_BUCKETS // 2 - 1
    j = pl.program_id(1)
    q0 = past + j * tq

    @pl.when((pl.program_id(0) == 0) & (j == 0))
    def _():
        for i in range(3):
            bkt = bkt_ref[i]
            for slot, h in enumerate(_HEAD_ORDER):
                t = jnp.zeros((lt, tq), F32)
                for b in range(NUM_BUCKETS):
                    t = jnp.where(bkt == b, rb_ref[b, h], t)
                tbl_ref[i, slot] = t - rb_ref[far_bucket, h]

    lo_rows = lax.broadcasted_iota(I32, (LANES, tq), 0) < HEAD_DIM
    for c in range(N_HEADS // 2):
        qi_t = qi_ref[:, c * LANES:(c + 1) * LANES].astype(F32).T
        q_t = q_ref[:, c * LANES:(c + 1) * LANES].astype(F32).T
        for half in range(2):
            h = 2 * c + half
            keep = lo_rows if half == 0 else jnp.logical_not(lo_rows)
            qit_ref[:, h * tq:(h + 1) * tq] = jnp.where(keep, qi_t, 0.0).astype(BF16)
            qt_ref[:, h * tq:(h + 1) * tq] = jnp.where(keep, q_t, 0.0).astype(BF16)

    wi_t = kw_ref[...].T[IDX_DIM:IDX_DIM + N_IDX_HEADS, :]
    wi_t = (wi_t * (N_IDX_HEADS ** -0.5)) * (IDX_DIM ** -0.5)

    qpos = q0 + lax.broadcasted_iota(I32, (1, tq), 1)
    n_adm = jnp.minimum(((qpos // CHUNK) + 1) * CHUNK, n_keys)
    n_adm_max = jnp.minimum(((q0 + tq - 1) // CHUNK + 1) * CHUNK, n_keys)
    n_kt = (n_adm_max + lt - 1) // lt
    row = lax.broadcasted_iota(I32, (lt, tq), 0)

    def score_body(kt, carry):
        base = pl.multiple_of(kt * lt, lt)
        s = jnp.dot(kk_ref[pl.ds(base, lt), :], qit_ref[...], preferred_element_type=F32)
        acc = jnp.zeros((lt, tq), F32)
        for h in range(N_IDX_HEADS):
            acc = acc + wi_t[h:h + 1, :] * jnp.maximum(s[:, h * tq:(h + 1) * tq], 0.0)
        sc_ref[pl.ds(base, lt), :] = jnp.where(row + base < n_adm, acc, -jnp.inf)
        return carry

    lax.fori_loop(0, n_kt, score_body, 0)

    def count(pred):
        def body(kt, cnt):
            base = pl.multiple_of(kt * lt, lt)
            hit = pred(sc_ref[pl.ds(base, lt), :], base).astype(I32)
            return cnt + jnp.sum(hit.reshape(lt // SUBLANES, SUBLANES, tq), axis=0)
        cnt = lax.fori_loop(0, n_kt, body, jnp.zeros((SUBLANES, tq), I32))
        return jnp.sum(cnt, axis=0, keepdims=True)

    def bisect(i, t):
        cand = t + (jnp.int32(1) << (31 - i))
        cand_f = _key_to_float(cand)
        return jnp.where(count(lambda blk, base: blk >= cand_f) >= top_k, cand, t)

    t_key = lax.fori_loop(0, 32, bisect, jnp.full((1, tq), INT_MIN, I32))
    thr = _key_to_float(t_key)
    n_gt = count(lambda blk, base: blk > thr)
    n_eq = count(lambda blk, base: blk == thr)
    want = top_k - n_gt
    finite_thr = thr > -jnp.inf
    need = (n_eq > want) & finite_thr
    tie_ref[...] = jnp.where(finite_thr, INT_MAX, -1)

    @pl.when(jnp.max(need.astype(I32)) > 0)
    def _():
        nbits = max(1, int(sc_ref.shape[0]).bit_length())

        def tie_step(i, m):
            cand = m | (jnp.int32(1) << (nbits - 1 - i))
            c = count(lambda blk, base: (blk == thr) & (row + base < cand))
            return jnp.where(c < want, cand, m)

        m = lax.fori_loop(0, nbits, tie_step, jnp.zeros((1, tq), I32))
        tie_ref[...] = jnp.where(need, m, tie_ref[...])

    tie_max = tie_ref[...]

    m_ref[...] = jnp.full(m_ref.shape, MASKED, F32)
    l_ref[...] = jnp.zeros(l_ref.shape, F32)
    acc_ref[...] = jnp.zeros(acc_ref.shape, F32)
    n_far = jnp.clip((q0 - (LANES - 1)) // lt, 0, n_kt)

    def attn_body(kt, near):
        base = pl.multiple_of(kt * lt, lt)
        blk = sc_ref[pl.ds(base, lt), :]
        sel = (blk > thr) | ((blk == thr) & (row + base <= tie_max))
        mask = jnp.where(sel, 0.0, MASKED)
        s = jnp.dot(kb_ref[pl.ds(base, lt), :], qt_ref[...], preferred_element_type=F32)
        parts = []
        for h in range(N_HEADS):
            sh = s[:, h * tq:(h + 1) * tq] + mask
            if near:
                sh = sh + tbl_ref[(base - q0) // LANES + 2, h]
            parts.append(sh)
        s = jnp.concatenate(parts, axis=1)
        m_old = m_ref[...]
        m_new = jnp.maximum(m_old, jnp.max(s, axis=0, keepdims=True))
        alpha = jnp.exp(m_old - m_new)
        p = jnp.exp(s - m_new)
        l_ref[...] = alpha * l_ref[...] + jnp.sum(p, axis=0, keepdims=True)
        pv = jnp.dot(vt_ref[kt], p.astype(BF16), preferred_element_type=F32)
        acc_ref[...] = alpha * acc_ref[...] + pv
        m_ref[...] = m_new

    def far_body(kt, carry):
        attn_body(kt, False)
        return carry

    def near_body(kt, carry):
        attn_body(kt, True)
        return carry

    lax.fori_loop(0, n_far, far_body, 0)
    lax.fori_loop(n_far, n_kt, near_body, 0)

    out_t = acc_ref[...] / l_ref[...]
    lo_lanes = lax.broadcasted_iota(I32, (tq, LANES), 1) < HEAD_DIM
    for c in range(N_HEADS // 2):
        lo = out_t[:, (2 * c) * tq:(2 * c + 1) * tq].T
        hi = out_t[:, (2 * c + 1) * tq:(2 * c + 2) * tq].T
        o_ref[:, c * LANES:(c + 1) * LANES] = jnp.where(lo_lanes, lo, hi).astype(o_ref.dtype)


def _attention(rel_bias, q, qi, kw, kk, kb, vb, *, past, n_keys):
    b, tq_total, _ = q.shape
    lp = kk.shape[1]
    top_k = min(TOPK_MAX, n_keys // 4)
    assert tq_total % Q_TILE == 0 and lp % KEY_TILE == 0 and past % KEY_TILE == 0
    assert lp >= n_keys and top_k <= KEY_TILE
    n_kt = lp // KEY_TILE
    vt = vb.reshape(b, n_kt, KEY_TILE, LANES).swapaxes(2, 3)
    bkt = jnp.asarray(_bucket_tables())
    qspec = lambda w: pl.BlockSpec((None, Q_TILE, w), lambda bi, j: (bi, j, 0))
    kspec = pl.BlockSpec((None, lp, LANES), lambda bi, j: (bi, 0, 0))
    return pl.pallas_call(
        functools.partial(_attn_kernel, past=past, n_keys=n_keys, top_k=top_k),
        grid=(b, tq_total // Q_TILE),
        in_specs=[pl.BlockSpec(memory_space=pltpu.SMEM),
                  pl.BlockSpec((3, KEY_TILE, Q_TILE), lambda bi, j: (0, 0, 0)),
                  qspec(ATTN_WIDTH), qspec(N_IDX_HEADS * IDX_DIM), qspec(LANES),
                  kspec, kspec,
                  pl.BlockSpec((None, n_kt, LANES, KEY_TILE), lambda bi, j: (bi, 0, 0, 0))],
        out_specs=qspec(ATTN_WIDTH),
        out_shape=jax.ShapeDtypeStruct((b, tq_total, ATTN_WIDTH), BF16),
        scratch_shapes=[
            pltpu.VMEM((lp, Q_TILE), F32),
            pltpu.VMEM((LANES, N_IDX_HEADS * Q_TILE), BF16),
            pltpu.VMEM((LANES, N_HEADS * Q_TILE), BF16),
            pltpu.VMEM((3, N_HEADS, KEY_TILE, Q_TILE), F32),
            pltpu.VMEM((1, N_HEADS * Q_TILE), F32),
            pltpu.VMEM((1, N_HEADS * Q_TILE), F32),
            pltpu.VMEM((LANES, N_HEADS * Q_TILE), F32),
            pltpu.VMEM((1, Q_TILE), I32),
        ],
        compiler_params=pltpu.CompilerParams(dimension_semantics=("arbitrary", "arbitrary"),
                                             vmem_limit_bytes=VMEM_LIMIT_BYTES),
        name="attn",
    )(rel_bias, bkt, q, qi, kw, kk, kb, vt)


def _softplus(x):
    return jnp.maximum(x, 0.0) + jnp.log1p(jnp.exp(-jnp.abs(x)))


def _rglru_kernel(xr_ref, gr_ref, h0_ref, c0_ref, cw_ref, cb_ref, wa_ref, ba_ref, wx_ref, bx_ref,
                  lam_ref, rnn_ref, hl_ref, xs_ref, a_ref, u_ref, hs_ref, hc_ref, *, tm):
    t = pl.program_id(1)
    pad = SUBLANES
    hist = RNN_CONV - 1

    @pl.when(t == 0)
    def _():
        xs_ref[pad - hist:pad, :] = c0_ref[...]
        hc_ref[...] = jnp.broadcast_to(h0_ref[...], hc_ref.shape)

    xs_ref[pad:pad + tm, :] = xr_ref[...]
    xc = cb_ref[...] + cw_ref[0:1, :] * xs_ref[pad - hist:pad - hist + tm, :]
    for jj in range(1, RNN_CONV):
        xc = xc + cw_ref[jj:jj + 1, :] * xs_ref[pad - hist + jj:pad - hist + jj + tm, :]
    xs_ref[pad - hist:pad, :] = xs_ref[pad + tm - hist:pad + tm, :]

    xcb = xc.astype(BF16)
    rg = jax.nn.sigmoid(jnp.dot(xcb, wa_ref[...], preferred_element_type=F32) + ba_ref[...])
    ig = jax.nn.sigmoid(jnp.dot(xcb, wx_ref[...], preferred_element_type=F32) + bx_ref[...])
    log_a = (-RGLRU_C * rg) * _softplus(-lam_ref[...])
    a = jnp.exp(log_a)
    u = jnp.sqrt(-jnp.tanh(log_a) * (jnp.exp(2.0 * log_a) + 1.0)) * (ig * xc)

    row = lax.broadcasted_iota(I32, a.shape, 0) % SUBLANES
    for s in (1, 2, 4):
        a_prev = pltpu.roll(a, s, 0)
        u_prev = pltpu.roll(u, s, 0)
        use = row >= s
        u = jnp.where(use, a * u_prev + u, u)
        a = jnp.where(use, a * a_prev, a)
    a_ref[...] = a
    u_ref[...] = u

    def carry_body(g, h):
        base = pl.multiple_of(g * SUBLANES, SUBLANES)
        hg = a_ref[pl.ds(base, SUBLANES), :] * h + u_ref[pl.ds(base, SUBLANES), :]
        hs_ref[pl.ds(base, SUBLANES), :] = hg
        return jnp.broadcast_to(hg[SUBLANES - 1:SUBLANES, :], hg.shape)

    h_end = lax.fori_loop(0, tm // SUBLANES, carry_body, hc_ref[...])
    hc_ref[...] = h_end
    rnn_ref[...] = (hs_ref[...] * jax.nn.gelu(gr_ref[...])).astype(rnn_ref.dtype)

    @pl.when(t == pl.num_programs(1) - 1)
    def _():
        hl_ref[...] = h_end[0:1, :]


def _rglru(xr, gr, h0, conv0, conv_w, conv_b, wa_bd, ba, wx_bd, bx, lam, tm):
    b, t, r = xr.shape
    assert t % tm == 0 and tm % SUBLANES == 0
    tile = pl.BlockSpec((None, tm, r), lambda bi, ti: (bi, ti, 0))
    per_b = lambda rows: pl.BlockSpec((None, rows, r), lambda bi, ti: (bi, 0, 0))
    full = lambda a: pl.BlockSpec(a.shape, lambda bi, ti: (0,) * a.ndim)
    return pl.pallas_call(
        functools.partial(_rglru_kernel, tm=tm),
        grid=(b, t // tm),
        in_specs=[tile, tile, per_b(1), per_b(RNN_CONV - 1), full(conv_w), full(conv_b),
                  full(wa_bd), full(ba), full(wx_bd), full(bx), full(lam)],
        out_specs=(tile, per_b(1)),
        out_shape=(jax.ShapeDtypeStruct((b, t, r), BF16), jax.ShapeDtypeStruct((b, 1, r), F32)),
        scratch_shapes=[pltpu.VMEM((tm + SUBLANES, r), F32), pltpu.VMEM((tm, r), F32),
                        pltpu.VMEM((tm, r), F32), pltpu.VMEM((tm, r), F32),
                        pltpu.VMEM((SUBLANES, r), F32)],
        compiler_params=pltpu.CompilerParams(dimension_semantics=("arbitrary", "arbitrary"),
                                             vmem_limit_bytes=VMEM_LIMIT_BYTES),
        name="rglru",
    )(xr, gr, h0, conv0, conv_w, conv_b, wa_bd, ba, wx_bd, bx, lam)


def _ffn_kernel(x_ref, at_ref, rn_ref, c0_ref, woa_ref, wor_ref, gf_ref, wu_ref, wg_ref, cw_ref,
                cb_ref, wd_ref, gl_ref, y_ref, fc_ref, up_ref, *, tm, final_norm):
    t = pl.program_id(1)
    pad = SUBLANES
    hist = FFN_CONV - 1

    @pl.when(t == 0)
    def _():
        up_ref[pad - hist:pad, :] = c0_ref[...]

    x1 = (x_ref[...]
          + jnp.dot(at_ref[...], woa_ref[...], preferred_element_type=F32)
          + jnp.dot(rn_ref[...], wor_ref[...], preferred_element_type=F32))
    f = _rmsnorm(x1, gf_ref[...]).astype(BF16)
    up_ref[pad:pad + tm, :] = jnp.dot(f, wu_ref[...], preferred_element_type=F32)
    up = cb_ref[...] + cw_ref[0:1, :] * up_ref[pad - hist:pad - hist + tm, :]
    for jj in range(1, FFN_CONV):
        up = up + cw_ref[jj:jj + 1, :] * up_ref[pad - hist + jj:pad - hist + jj + tm, :]
    tail = up_ref[pad + tm - hist:pad + tm, :]
    up_ref[pad - hist:pad, :] = tail
    gate = jnp.dot(f, wg_ref[...], preferred_element_type=F32)
    act = (jax.nn.gelu(up) * gate).astype(BF16)
    x2 = x1 + jnp.dot(act, wd_ref[...], preferred_element_type=F32)
    y_ref[...] = _rmsnorm(x2, gl_ref[...]) if final_norm else x2

    @pl.when(t == pl.num_programs(1) - 1)
    def _():
        fc_ref[...] = tail


def _ffn(x, attn, rnn, conv0, wo_a, wo_r, g_ffn, w_up, w_gate, conv_w, conv_b, w_down, g_last,
         tm, final_norm):
    b, t, d = x.shape
    dff = w_up.shape[1]
    assert t % tm == 0
    tile = lambda w: pl.BlockSpec((None, tm, w), lambda bi, ti: (bi, ti, 0))
    per_b = pl.BlockSpec((None, FFN_CONV - 1, dff), lambda bi, ti: (bi, 0, 0))
    full = lambda a: pl.BlockSpec(a.shape, lambda bi, ti: (0,) * a.ndim,
                                  pipeline_mode=pl.Buffered(1))
    return pl.pallas_call(
        functools.partial(_ffn_kernel, tm=tm, final_norm=final_norm),
        grid=(b, t // tm),
        in_specs=[tile(d), tile(attn.shape[-1]), tile(rnn.shape[-1]), per_b, full(wo_a), full(wo_r),
                  full(g_ffn), full(w_up), full(w_gate), full(conv_w), full(conv_b), full(w_down),
                  full(g_last)],
        out_specs=(tile(d), per_b),
        out_shape=(jax.ShapeDtypeStruct((b, t, d), F32),
                   jax.ShapeDtypeStruct((b, FFN_CONV - 1, dff), F32)),
        scratch_shapes=[pltpu.VMEM((tm + SUBLANES, dff), F32)],
        compiler_params=pltpu.CompilerParams(dimension_semantics=("arbitrary", "arbitrary"),
                                             vmem_limit_bytes=VMEM_LIMIT_BYTES),
        name="ffn",
    )(x, attn, rnn, conv0, wo_a, wo_r, g_ffn, w_up, w_gate, conv_w, conv_b, w_down, g_last)


_HEAD_ORDER = tuple(h for c in range(GQA_GROUP) for h in (c, GQA_GROUP + c))


def _pack_layer_weights(w_in, wa, wx, w_out, w_up, w_gate, w_down):
    d = w_in.shape[0]
    sizes = [ATTN_WIDTH, N_KV_HEADS * HEAD_DIM, N_KV_HEADS * HEAD_DIM, N_IDX_HEADS * IDX_DIM,
             IDX_DIM, N_IDX_HEADS]
    offs = np.cumsum([0] + sizes)
    wq, wk, wv, wqi, wki, wwi = (w_in[:, offs[i]:offs[i + 1]] for i in range(6))
    wrest = w_in[:, offs[6]:]
    rnn_width = wrest.shape[1] // 2
    order = np.asarray(_HEAD_ORDER)
    wq = wq.reshape(d, N_HEADS, HEAD_DIM)[:, order, :].reshape(d, ATTN_WIDTH)
    zpad = jnp.zeros((d, LANES - IDX_DIM - N_IDX_HEADS), w_in.dtype)
    packed = jnp.concatenate([wq, wk, wv, wqi, wki, wwi, zpad, wki, wki, wrest], axis=1).astype(BF16)
    eye = jnp.eye(RNN_BLOCKS, dtype=wa.dtype)
    bd = lambda w: jnp.einsum("nij,nm->nimj", w, eye).reshape(rnn_width, rnn_width).astype(BF16)
    wo_a = w_out[:ATTN_WIDTH].reshape(N_HEADS, HEAD_DIM, -1)[order].reshape(ATTN_WIDTH, -1)
    wo_r = w_out[ATTN_WIDTH:]
    return dict(w_packed=packed, rnn_width=rnn_width, wa_bd=bd(wa), wx_bd=bd(wx),
                wo_a=wo_a.astype(BF16), wo_r=wo_r.astype(BF16), w_up=w_up.astype(BF16),
                w_gate=w_gate.astype(BF16), w_down=w_down.astype(BF16))


def _pad_rows(a, rows):
    return a if a.shape[1] == rows else jnp.pad(a, ((0, 0), (0, rows - a.shape[1]), (0, 0)))


def _round_up(n, m):
    return -(-n // m) * m


def _layer(x, ck, cv, ckidx, h0, rconv0, fconv0, pw, norm_mix, conv_w, conv_b, ba, bx, lam,
           rel_bias, norm_ffn, fconv_w, fconv_b, norm_last, final_norm):
    b, t, d = x.shape
    past = ck.shape[1]
    r = pw["rnn_width"]
    n_tok = b * t
    tm = min(512, n_tok)
    (q, k, v, kb, vb, qi, ki, kk, kw, xr, gr) = _project(
        x.reshape(n_tok, d), norm_mix[None, :], pw["w_packed"], r, tm)
    b3 = lambda a: a.reshape(b, t, a.shape[-1])

    n_keys = past + t
    lp = _round_up(n_keys, KEY_TILE)
    tq = _round_up(t, Q_TILE)
    if past:
        ck2 = ck.reshape(b, past, -1).astype(BF16)
        cv2 = cv.reshape(b, past, -1).astype(BF16)
        cki = ckidx.astype(BF16)
        keys_k = jnp.concatenate([ck2, b3(kb)], axis=1)
        keys_v = jnp.concatenate([cv2, b3(vb)], axis=1)
        keys_i = jnp.concatenate([jnp.concatenate([cki, cki], axis=-1), b3(kk)], axis=1)
    else:
        keys_k, keys_v, keys_i = b3(kb), b3(vb), b3(kk)
    attn = _attention(rel_bias, _pad_rows(b3(q), tq), _pad_rows(b3(qi), tq), _pad_rows(b3(kw), tq),
                      _pad_rows(keys_i, lp), _pad_rows(keys_k, lp), _pad_rows(keys_v, lp),
                      past=past, n_keys=n_keys)[:, :t]

    tr = min(512, t)
    rnn, h_last = _rglru(b3(xr), b3(gr), h0[:, None, :], rconv0, conv_w, conv_b[None, :],
                         pw["wa_bd"], ba[None, :], pw["wx_bd"], bx[None, :], lam[None, :], tr)
    assert t >= RNN_CONV - 1 and t >= FFN_CONV - 1
    rconv_new = b3(xr)[:, t - (RNN_CONV - 1):, :]

    tf = min(256, t)
    y, fconv_new = _ffn(x, attn, rnn, fconv0, pw["wo_a"], pw["wo_r"], norm_ffn[None, :], pw["w_up"],
                        pw["w_gate"], fconv_w, fconv_b[None, :], pw["w_down"], norm_last[None, :],
                        tf, final_norm)
    k_new = k.reshape(b, t, N_KV_HEADS, HEAD_DIM)
    v_new = v.reshape(b, t, N_KV_HEADS, HEAD_DIM)
    return y, (k_new, v_new, b3(ki), h_last[:, 0, :], rconv_new, fconv_new)


def kernel(x_prompt, x_sample, cache_k, cache_v, cache_kidx, state_rglru_h, state_rglru_conv,
           state_ffn_conv, norm_mix, w_in, rglru_conv_w, rglru_conv_b, rglru_wa, rglru_ba,
           rglru_wx, rglru_bx, rglru_lambda, rel_bias, w_out, norm_ffn, w_ffn_up, w_ffn_gate,
           ffn_conv_w, ffn_conv_b, w_ffn_down, norm_final):
    depth = w_in.shape[0]
    bp = x_prompt.shape[0]
    dt = x_prompt.dtype
    xp, xs = x_prompt, x_sample
    outs_p, outs_s = [], []
    for i in range(depth):
        pw = _pack_layer_weights(w_in[i], rglru_wa[i], rglru_wx[i], w_out[i], w_ffn_up[i],
                                 w_ffn_gate[i], w_ffn_down[i])
        r = pw["rnn_width"]
        last = i == depth - 1
        lw = (pw, norm_mix[i], rglru_conv_w[i], rglru_conv_b[i], rglru_ba[i], rglru_bx[i],
              rglru_lambda[i], rel_bias, norm_ffn[i], ffn_conv_w[i], ffn_conv_b[i], norm_final, last)
        xp, st_p = _layer(xp,
                          jnp.zeros((bp, 0, N_KV_HEADS, HEAD_DIM), dt),
                          jnp.zeros((bp, 0, N_KV_HEADS, HEAD_DIM), dt),
                          jnp.zeros((bp, 0, IDX_DIM), dt),
                          jnp.zeros((bp, r), dt),
                          jnp.zeros((bp, RNN_CONV - 1, r), dt),
                          jnp.zeros((bp, FFN_CONV - 1, w_ffn_up.shape[-1]), dt),
                          *lw)
        xs, st_s = _layer(xs, cache_k[i], cache_v[i], cache_kidx[i], state_rglru_h[i],
                          state_rglru_conv[i], state_ffn_conv[i], *lw)
        outs_p.append(st_p)
        outs_s.append(st_s)
    stack = lambda outs, j: jnp.stack([o[j] for o in outs])
    return ((xp, xs) + tuple(stack(outs_p, j) for j in range(6))
            + tuple(stack(outs_s, j) for j in range(6)))
```

```python
import functools
import math

import jax
import jax.numpy as jnp
import numpy as np
from jax import lax
from jax.experimental import pallas as pl
from jax.experimental.pallas import tpu as pltpu

F32 = jnp.float32
BF16 = jnp.bfloat16
I32 = jnp.int32
I16 = jnp.int16

CHUNK = 64
N_HEADS = 8
HEAD_DIM = 64
N_KV_HEADS = 2
GQA_GROUP = N_HEADS // N_KV_HEADS
ATTN_WIDTH = N_HEADS * HEAD_DIM
N_IDX_HEADS = 8
IDX_DIM = 64
TOPK_MAX = 256
NUM_BUCKETS = 32
MAX_DISTANCE = 128
RNN_BLOCKS = 8
RNN_CONV = 4
RGLRU_C = 8.0
FFN_CONV = 3
EPS = 1e-6

LANES = 128
SUBLANES = 8
PACK16 = 2 * SUBLANES
VMEM_LIMIT_BYTES = 56 * 1024 * 1024

Q_TILE = 128
KEY_TILE = 256
SEL_CHUNK = 2 * KEY_TILE
MASKED = -1e30
INT_MIN = -(2 ** 31)
INT_MAX = 2 ** 31 - 1
HALF16 = 2 ** 15

_C_Q = 0
_C_K = _C_Q + ATTN_WIDTH
_C_V = _C_K + LANES
_C_QI = _C_V + LANES
_C_KW = _C_QI + N_IDX_HEADS * IDX_DIM
_C_KK = _C_KW + LANES
_C_XR = _C_KK + LANES


def _rmsnorm(x, g):
    y = x * lax.rsqrt(jnp.mean(x * x, axis=-1, keepdims=True) + EPS)
    return y * g


def _proj_kernel(x_ref, g_ref, w_ref, q_ref, k_ref, v_ref, kb_ref, vb_ref, qi_ref, ki_ref,
                 kk_ref, kw_ref, xr_ref, gr_ref, *, rnn_width):
    h = _rmsnorm(x_ref[...], g_ref[...]).astype(BF16)

    def proj(start, width):
        return jnp.dot(h, w_ref[:, start:start + width], preferred_element_type=F32)

    q_ref[...] = (proj(_C_Q, ATTN_WIDTH) * (HEAD_DIM ** -0.5)).astype(BF16)
    k = proj(_C_K, LANES)
    v = proj(_C_V, LANES)
    k_ref[...] = k
    v_ref[...] = v
    kb_ref[...] = k.astype(BF16)
    vb_ref[...] = v.astype(BF16)
    qi_ref[...] = proj(_C_QI, N_IDX_HEADS * IDX_DIM).astype(BF16)
    kw = proj(_C_KW, LANES)
    kw_ref[...] = kw
    ki_ref[...] = kw[:, :IDX_DIM]
    kk_ref[...] = proj(_C_KK, LANES).astype(BF16)
    xr_ref[...] = proj(_C_XR, rnn_width)
    gr_ref[...] = proj(_C_XR + rnn_width, rnn_width)


def _project(x2d, g, w_packed, rnn_width, tm):
    n, d = x2d.shape
    ncol = w_packed.shape[1]
    row = lambda w: pl.BlockSpec((tm, w), lambda i: (i, 0))
    out_shape = (
        jax.ShapeDtypeStruct((n, ATTN_WIDTH), BF16),
        jax.ShapeDtypeStruct((n, LANES), F32),
        jax.ShapeDtypeStruct((n, LANES), F32),
        jax.ShapeDtypeStruct((n, LANES), BF16),
        jax.ShapeDtypeStruct((n, LANES), BF16),
        jax.ShapeDtypeStruct((n, N_IDX_HEADS * IDX_DIM), BF16),
        jax.ShapeDtypeStruct((n, IDX_DIM), F32),
        jax.ShapeDtypeStruct((n, LANES), BF16),
        jax.ShapeDtypeStruct((n, LANES), F32),
        jax.ShapeDtypeStruct((n, rnn_width), F32),
        jax.ShapeDtypeStruct((n, rnn_width), F32),
    )
    out_specs = (row(ATTN_WIDTH), row(LANES), row(LANES), row(LANES), row(LANES),
                 row(N_IDX_HEADS * IDX_DIM), row(IDX_DIM), row(LANES), row(LANES),
                 row(rnn_width), row(rnn_width))
    return pl.pallas_call(
        functools.partial(_proj_kernel, rnn_width=rnn_width),
        grid=(n // tm,),
        in_specs=[row(d), pl.BlockSpec((1, d), lambda i: (0, 0)),
                  pl.BlockSpec((d, ncol), lambda i: (0, 0))],
        out_specs=out_specs,
        out_shape=out_shape,
        compiler_params=pltpu.CompilerParams(dimension_semantics=("arbitrary",),
                                             vmem_limit_bytes=VMEM_LIMIT_BYTES),
        name="proj",
    )(x2d, g, w_packed)


def _bucket_tables():
    half = NUM_BUCKETS // 2
    max_exact = half // 2
    r = np.arange(Q_TILE, dtype=np.int64)[None, :]
    c = np.arange(KEY_TILE, dtype=np.int64)[:, None]
    tabs = []
    for i in range(3):
        rel = (i - 2) * LANES + c - r
        side = np.where(rel > 0, half, 0)
        n = np.abs(rel)
        nf = np.maximum(n, 1).astype(np.float32)
        ratio = np.log(nf / np.float32(max_exact)) / np.float32(math.log(MAX_DISTANCE / max_exact))
        large = max_exact + (ratio * np.float32(half - max_exact)).astype(np.int32)
        large = np.minimum(large, half - 1)
        tabs.append(side + np.where(n < max_exact, n, large))
    return np.stack(tabs).astype(np.int32)


def _key_to_float(k):
    bits = jnp.where(k < 0, (-k) | INT_MIN, k)
    return lax.bitcast_convert_type(bits, F32)


def _attn_kernel(rb_ref, bkt_ref, q_ref, qi_ref, kw_ref, kk_ref, kb_ref, vt_ref, o_ref,
                 sc_ref, hi_ref, lo_ref, qit_ref, qt_ref, tbl_ref, s_ref, mx_ref, den_ref, acc_ref, tie_ref,
                 *, past, n_keys, top_k):
    tq, lt, ct = Q_TILE, KEY_TILE, SEL_CHUNK
    far_bucket = NUM_BUCKETS // 2 - 1
    j = pl.program_id(1)
    q0 = past + j * tq

    @pl.when((pl.program_id(0) == 0) & (j == 0))
    def _():
        for i in range(3):
            bkt = bkt_ref[i]
            for slot, h in enumerate(_HEAD_ORDER):
                t = jnp.zeros((lt, tq), F32)
                for b in range(NUM_BUCKETS):
                    t = jnp.where(bkt == b, rb_ref[b, h], t)
                tbl_ref[i, slot] = t - rb_ref[far_bucket, h]

    lo_rows = lax.broadcasted_iota(I32, (LANES, tq), 0) < HEAD_DIM
    for c in range(N_HEADS // 2):
        qi_t = qi_ref[:, c * LANES:(c + 1) * LANES].astype(F32).T
        q_t = q_ref[:, c * LANES:(c + 1) * LANES].astype(F32).T
        for half in range(2):
            h = 2 * c + half
            keep = lo_rows if half == 0 else jnp.logical_not(lo_rows)
            qit_ref[:, h * tq:(h + 1) * tq] = jnp.where(keep, qi_t, 0.0).astype(BF16)
            qt_ref[:, h * tq:(h + 1) * tq] = jnp.where(keep, q_t, 0.0).astype(BF16)

    wi_t = kw_ref[...].T[IDX_DIM:IDX_DIM + N_IDX_HEADS, :]
    wi_t = (wi_t * (N_IDX_HEADS ** -0.5)) * (IDX_DIM ** -0.5)

    qpos = q0 + lax.broadcasted_iota(I32, (1, tq), 1)
    n_adm = jnp.minimum(((qpos // CHUNK) + 1) * CHUNK, n_keys)
    n_adm_max = jnp.minimum(((q0 + tq - 1) // CHUNK + 1) * CHUNK, n_keys)
    n_kt = (n_adm_max + lt - 1) // lt
    n_ct = (n_adm_max + ct - 1) // ct
    crow = lax.broadcasted_iota(I32, (ct, tq), 0)

    def store_scores(base, sc):
        rows = sc.shape[0]
        sc_ref[pl.ds(base, rows), :] = sc
        bits = lax.bitcast_convert_type(sc, I32)
        key = jnp.where(bits < 0, INT_MIN - bits, bits)
        hi_ref[pl.ds(base, rows), :] = (key >> 16).astype(I16)
        lo_ref[pl.ds(base, rows), :] = ((key & 0xFFFF) - HALF16).astype(I16)

    def for_tiles(first, stop, span_fn):
        n = stop - first

        def quads(i, carry):
            span_fn(pl.multiple_of((first + 4 * i) * lt, lt), 4)
            return carry

        lax.fori_loop(0, n // 4, quads, 0)

        @pl.when((n & 2) != 0)
        def _():
            span_fn(pl.multiple_of((first + (n // 4) * 4) * lt, lt), 2)

        @pl.when((n & 1) != 0)
        def _():
            span_fn(pl.multiple_of((stop - 1) * lt, lt), 1)

    def score_span(base, tiles):
        rows = tiles * lt
        s = jnp.dot(kk_ref[pl.ds(base, rows), :], qit_ref[...], preferred_element_type=F32)
        acc = jnp.zeros((rows, tq), F32)
        for h in range(N_IDX_HEADS):
            acc = acc + wi_t[h:h + 1, :] * jnp.maximum(s[:, h * tq:(h + 1) * tq], 0.0)
        key_pos = lax.broadcasted_iota(I32, (rows, tq), 0) + base
        store_scores(base, jnp.where(key_pos < n_adm, acc, -jnp.inf))

    for_tiles(0, n_kt, score_span)

    @pl.when(n_kt * lt < n_ct * ct)
    def _():
        store_scores(pl.multiple_of(n_kt * lt, lt), jnp.full((lt, tq), -jnp.inf, F32))

    def count16(ref, cand):
        cand16 = cand.astype(I16)

        def body(c, acc):
            base = pl.multiple_of(c * ct, ct)
            hit = jnp.where(ref[pl.ds(base, ct), :] >= cand16, jnp.int16(1), jnp.int16(0))
            parts = [hit[r * PACK16:(r + 1) * PACK16] for r in range(ct // PACK16)]
            while len(parts) > 1:
                parts = [a + b for a, b in zip(parts[0::2], parts[1::2])]
            return acc + parts[0]

        acc = lax.fori_loop(0, n_ct, body, jnp.zeros((PACK16, tq), I16))
        return jnp.sum(acc.astype(I32), axis=0, keepdims=True)

    def bisect16(ref, k_needed):
        def step(i, t):
            cand = t + (jnp.int32(1) << (15 - i))
            return jnp.where(count16(ref, cand) >= k_needed, cand, t)
        return lax.fori_loop(0, 16, step, jnp.full((1, tq), -HALF16, I32))

    t_hi = bisect16(hi_ref, top_k)
    above = count16(hi_ref, t_hi + 1)
    t_hi16 = t_hi.astype(I16)

    def keep_bin(c, carry):
        base = pl.multiple_of(c * ct, ct)
        in_bin = hi_ref[pl.ds(base, ct), :] == t_hi16
        lo_ref[pl.ds(base, ct), :] = jnp.where(in_bin, lo_ref[pl.ds(base, ct), :], jnp.int16(-HALF16))
        return carry

    lax.fori_loop(0, n_ct, keep_bin, 0)
    t_lo = bisect16(lo_ref, top_k - above)
    thr = _key_to_float((t_hi << 16) + (t_lo + HALF16))

    def count32(pred):
        def body(c, cnt):
            base = pl.multiple_of(c * ct, ct)
            hit = pred(sc_ref[pl.ds(base, ct), :], base).astype(I32)
            return cnt + jnp.sum(hit.reshape(ct // SUBLANES, SUBLANES, tq), axis=0)
        cnt = lax.fori_loop(0, n_ct, body, jnp.zeros((SUBLANES, tq), I32))
        return jnp.sum(cnt, axis=0, keepdims=True)

    n_gt = count32(lambda blk, base: blk > thr)
    n_eq = count32(lambda blk, base: blk == thr)
    want = top_k - n_gt
    finite_thr = thr > -jnp.inf
    need = (n_eq > want) & finite_thr
    tie_ref[...] = jnp.where(finite_thr, INT_MAX, -1)

    @pl.when(jnp.max(need.astype(I32)) > 0)
    def _():
        nbits = max(1, int(sc_ref.shape[0]).bit_length())

        def tie_step(i, m):
            cand = m | (jnp.int32(1) << (nbits - 1 - i))
            c = count32(lambda blk, base: (blk == thr) & (crow + base < cand))
            return jnp.where(c < want, cand, m)

        m = lax.fori_loop(0, nbits, tie_step, jnp.zeros((1, tq), I32))
        tie_ref[...] = jnp.where(need, m, tie_ref[...])

    tie_max = tie_ref[...]

    n_far = jnp.clip((q0 - (LANES - 1)) // lt, 0, n_kt)

    def logits_span(base, tiles, near=False):
        rows = tiles * lt
        blk = sc_ref[pl.ds(base, rows), :]
        key_pos = lax.broadcasted_iota(I32, (rows, tq), 0) + base
        sel = (blk > thr) | ((blk == thr) & (key_pos <= tie_max))
        mask = jnp.where(sel, 0.0, MASKED)
        s = jnp.dot(kb_ref[pl.ds(base, rows), :], qt_ref[...], preferred_element_type=F32)
        tops = []
        for h in range(N_HEADS):
            sh = s[:, h * tq:(h + 1) * tq] + mask
            if near:
                sh = sh + tbl_ref[(base - q0) // LANES + 2, h]
            s_ref[pl.ds(base, rows), h * tq:(h + 1) * tq] = sh
            tops.append(jnp.max(sh.reshape(rows // SUBLANES, SUBLANES, tq), axis=0))
        mx_ref[...] = jnp.maximum(mx_ref[...], jnp.concatenate(tops, axis=1))

    def near_tile(kt, carry):
        logits_span(pl.multiple_of(kt * lt, lt), 1, near=True)
        return carry

    mx_ref[...] = jnp.full(mx_ref.shape, MASKED, F32)
    for_tiles(0, n_far, logits_span)
    lax.fori_loop(n_far, n_kt, near_tile, 0)
    m_fin = jnp.max(mx_ref[...], axis=0, keepdims=True)

    acc_ref[...] = jnp.zeros(acc_ref.shape, F32)
    den_ref[...] = jnp.zeros(den_ref.shape, F32)

    def pv_span(base, tiles):
        rows = tiles * lt
        p = jnp.exp(s_ref[pl.ds(base, rows), :] - m_fin)
        kt = base // lt
        values_t = jnp.concatenate([vt_ref[kt + t] for t in range(tiles)], axis=1)
        acc_ref[...] += jnp.dot(values_t, p.astype(BF16), preferred_element_type=F32)
        den_ref[...] += jnp.sum(p.reshape(rows // SUBLANES, SUBLANES, N_HEADS * tq), axis=0)

    for_tiles(0, n_kt, pv_span)
    out_t = acc_ref[...] / jnp.sum(den_ref[...], axis=0, keepdims=True)
    lo_lanes = lax.broadcasted_iota(I32, (tq, LANES), 1) < HEAD_DIM
    for c in range(N_HEADS // 2):
        lo = out_t[:, (2 * c) * tq:(2 * c + 1) * tq].T
        hi = out_t[:, (2 * c + 1) * tq:(2 * c + 2) * tq].T
        o_ref[:, c * LANES:(c + 1) * LANES] = jnp.where(lo_lanes, lo, hi).astype(o_ref.dtype)


def _attention(rel_bias, q, qi, kw, kk, kb, vb, *, past, n_keys):
    b, tq_total, _ = q.shape
    lp = kk.shape[1]
    top_k = min(TOPK_MAX, n_keys // 4)
    assert tq_total % Q_TILE == 0 and lp % KEY_TILE == 0 and past % KEY_TILE == 0
    assert lp >= n_keys and top_k <= KEY_TILE
    n_kt = lp // KEY_TILE
    lc = _round_up(lp, SEL_CHUNK)
    vt = vb.reshape(b, n_kt, KEY_TILE, LANES).swapaxes(2, 3)
    bkt = jnp.asarray(_bucket_tables())
    qspec = lambda w: pl.BlockSpec((None, Q_TILE, w), lambda bi, j: (bi, j, 0))
    kspec = pl.BlockSpec((None, lp, LANES), lambda bi, j: (bi, 0, 0))
    return pl.pallas_call(
        functools.partial(_attn_kernel, past=past, n_keys=n_keys, top_k=top_k),
        grid=(b, tq_total // Q_TILE),
        in_specs=[pl.BlockSpec(memory_space=pltpu.SMEM),
                  pl.BlockSpec((3, KEY_TILE, Q_TILE), lambda bi, j: (0, 0, 0)),
                  qspec(ATTN_WIDTH), qspec(N_IDX_HEADS * IDX_DIM), qspec(LANES),
                  kspec, kspec,
                  pl.BlockSpec((None, n_kt, LANES, KEY_TILE), lambda bi, j: (bi, 0, 0, 0))],
        out_specs=qspec(ATTN_WIDTH),
        out_shape=jax.ShapeDtypeStruct((b, tq_total, ATTN_WIDTH), BF16),
        scratch_shapes=[
            pltpu.VMEM((lc, Q_TILE), F32),
            pltpu.VMEM((lc, Q_TILE), I16),
            pltpu.VMEM((lc, Q_TILE), I16),
            pltpu.VMEM((LANES, N_IDX_HEADS * Q_TILE), BF16),
            pltpu.VMEM((LANES, N_HEADS * Q_TILE), BF16),
            pltpu.VMEM((3, N_HEADS, KEY_TILE, Q_TILE), F32),
            pltpu.VMEM((lp, N_HEADS * Q_TILE), F32),
            pltpu.VMEM((SUBLANES, N_HEADS * Q_TILE), F32),
            pltpu.VMEM((SUBLANES, N_HEADS * Q_TILE), F32),
            pltpu.VMEM((LANES, N_HEADS * Q_TILE), F32),
            pltpu.VMEM((1, Q_TILE), I32),
        ],
        compiler_params=pltpu.CompilerParams(dimension_semantics=("arbitrary", "arbitrary"),
                                             vmem_limit_bytes=VMEM_LIMIT_BYTES),
        name="attn",
    )(rel_bias, bkt, q, qi, kw, kk, kb, vt)


def _softplus(x):
    return jnp.maximum(x, 0.0) + jnp.log1p(jnp.exp(-jnp.abs(x)))


def _rglru_kernel(xr_ref, gr_ref, h0_ref, c0_ref, cw_ref, cb_ref, wa_ref, ba_ref, wx_ref, bx_ref,
                  lam_ref, rnn_ref, hl_ref, xs_ref, a_ref, u_ref, hs_ref, hc_ref, *, tm):
    t = pl.program_id(1)
    pad = SUBLANES
    hist = RNN_CONV - 1

    @pl.when(t == 0)
    def _():
        xs_ref[pad - hist:pad, :] = c0_ref[...]
        hc_ref[...] = jnp.broadcast_to(h0_ref[...], hc_ref.shape)

    xs_ref[pad:pad + tm, :] = xr_ref[...]
    xc = cb_ref[...] + cw_ref[0:1, :] * xs_ref[pad - hist:pad - hist + tm, :]
    for jj in range(1, RNN_CONV):
        xc = xc + cw_ref[jj:jj + 1, :] * xs_ref[pad - hist + jj:pad - hist + jj + tm, :]
    xs_ref[pad - hist:pad, :] = xs_ref[pad + tm - hist:pad + tm, :]

    xcb = xc.astype(BF16)
    rg = jax.nn.sigmoid(jnp.dot(xcb, wa_ref[...], preferred_element_type=F32) + ba_ref[...])
    ig = jax.nn.sigmoid(jnp.dot(xcb, wx_ref[...], preferred_element_type=F32) + bx_ref[...])
    log_a = (-RGLRU_C * rg) * _softplus(-lam_ref[...])
    a = jnp.exp(log_a)
    u = jnp.sqrt(-jnp.tanh(log_a) * (jnp.exp(2.0 * log_a) + 1.0)) * (ig * xc)

    row = lax.broadcasted_iota(I32, a.shape, 0) % SUBLANES
    for s in (1, 2, 4):
        a_prev = pltpu.roll(a, s, 0)
        u_prev = pltpu.roll(u, s, 0)
        use = row >= s
        u = jnp.where(use, a * u_prev + u, u)
        a = jnp.where(use, a * a_prev, a)
    a_ref[...] = a
    u_ref[...] = u

    def carry_body(g, h):
        base = pl.multiple_of(g * SUBLANES, SUBLANES)
        hg = a_ref[pl.ds(base, SUBLANES), :] * h + u_ref[pl.ds(base, SUBLANES), :]
        hs_ref[pl.ds(base, SUBLANES), :] = hg
        return jnp.broadcast_to(hg[SUBLANES - 1:SUBLANES, :], hg.shape)

    h_end = lax.fori_loop(0, tm // SUBLANES, carry_body, hc_ref[...])
    hc_ref[...] = h_end
    rnn_ref[...] = (hs_ref[...] * jax.nn.gelu(gr_ref[...])).astype(rnn_ref.dtype)

    @pl.when(t == pl.num_programs(1) - 1)
    def _():
        hl_ref[...] = h_end[0:1, :]


def _rglru(xr, gr, h0, conv0, conv_w, conv_b, wa_bd, ba, wx_bd, bx, lam, tm):
    b, t, r = xr.shape
    assert t % tm == 0 and tm % SUBLANES == 0
    tile = pl.BlockSpec((None, tm, r), lambda bi, ti: (bi, ti, 0))
    per_b = lambda rows: pl.BlockSpec((None, rows, r), lambda bi, ti: (bi, 0, 0))
    full = lambda a: pl.BlockSpec(a.shape, lambda bi, ti: (0,) * a.ndim)
    return pl.pallas_call(
        functools.partial(_rglru_kernel, tm=tm),
        grid=(b, t // tm),
        in_specs=[tile, tile, per_b(1), per_b(RNN_CONV - 1), full(conv_w), full(conv_b),
                  full(wa_bd), full(ba), full(wx_bd), full(bx), full(lam)],
        out_specs=(tile, per_b(1)),
        out_shape=(jax.ShapeDtypeStruct((b, t, r), BF16), jax.ShapeDtypeStruct((b, 1, r), F32)),
        scratch_shapes=[pltpu.VMEM((tm + SUBLANES, r), F32), pltpu.VMEM((tm, r), F32),
                        pltpu.VMEM((tm, r), F32), pltpu.VMEM((tm, r), F32),
                        pltpu.VMEM((SUBLANES, r), F32)],
        compiler_params=pltpu.CompilerParams(dimension_semantics=("arbitrary", "arbitrary"),
                                             vmem_limit_bytes=VMEM_LIMIT_BYTES),
        name="rglru",
    )(xr, gr, h0, conv0, conv_w, conv_b, wa_bd, ba, wx_bd, bx, lam)


def _ffn_kernel(x_ref, at_ref, rn_ref, c0_ref, woa_ref, wor_ref, gf_ref, wu_ref, wg_ref, cw_ref,
                cb_ref, wd_ref, gl_ref, y_ref, fc_ref, up_ref, *, tm, final_norm):
    t = pl.program_id(1)
    pad = SUBLANES
    hist = FFN_CONV - 1

    @pl.when(t == 0)
    def _():
        up_ref[pad - hist:pad, :] = c0_ref[...]

    x1 = (x_ref[...]
          + jnp.dot(at_ref[...], woa_ref[...], preferred_element_type=F32)
          + jnp.dot(rn_ref[...], wor_ref[...], preferred_element_type=F32))
    f = _rmsnorm(x1, gf_ref[...]).astype(BF16)
    up_ref[pad:pad + tm, :] = jnp.dot(f, wu_ref[...], preferred_element_type=F32)
    up = cb_ref[...] + cw_ref[0:1, :] * up_ref[pad - hist:pad - hist + tm, :]
    for jj in range(1, FFN_CONV):
        up = up + cw_ref[jj:jj + 1, :] * up_ref[pad - hist + jj:pad - hist + jj + tm, :]
    tail = up_ref[pad + tm - hist:pad + tm, :]
    up_ref[pad - hist:pad, :] = tail
    gate = jnp.dot(f, wg_ref[...], preferred_element_type=F32)
    act = (jax.nn.gelu(up) * gate).astype(BF16)
    x2 = x1 + jnp.dot(act, wd_ref[...], preferred_element_type=F32)
    y_ref[...] = _rmsnorm(x2, gl_ref[...]) if final_norm else x2

    @pl.when(t == pl.num_programs(1) - 1)
    def _():
        fc_ref[...] = tail


def _ffn(x, attn, rnn, conv0, wo_a, wo_r, g_ffn, w_up, w_gate, conv_w, conv_b, w_down, g_last,
         tm, final_norm):
    b, t, d = x.shape
    dff = w_up.shape[1]
    assert t % tm == 0
    tile = lambda w: pl.BlockSpec((None, tm, w), lambda bi, ti: (bi, ti, 0))
    per_b = pl.BlockSpec((None, FFN_CONV - 1, dff), lambda bi, ti: (bi, 0, 0))
    full = lambda a: pl.BlockSpec(a.shape, lambda bi, ti: (0,) * a.ndim,
                                  pipeline_mode=pl.Buffered(1))
    return pl.pallas_call(
        functools.partial(_ffn_kernel, tm=tm, final_norm=final_norm),
        grid=(b, t // tm),
        in_specs=[tile(d), tile(attn.shape[-1]), tile(rnn.shape[-1]), per_b, full(wo_a), full(wo_r),
                  full(g_ffn), full(w_up), full(w_gate), full(conv_w), full(conv_b), full(w_down),
                  full(g_last)],
        out_specs=(tile(d), per_b),
        out_shape=(jax.ShapeDtypeStruct((b, t, d), F32),
                   jax.ShapeDtypeStruct((b, FFN_CONV - 1, dff), F32)),
        scratch_shapes=[pltpu.VMEM((tm + SUBLANES, dff), F32)],
        compiler_params=pltpu.CompilerParams(dimension_semantics=("arbitrary", "arbitrary"),
                                             vmem_limit_bytes=VMEM_LIMIT_BYTES),
        name="ffn",
    )(x, attn, rnn, conv0, wo_a, wo_r, g_ffn, w_up, w_gate, conv_w, conv_b, w_down, g_last)


_HEAD_ORDER = tuple(h for c in range(GQA_GROUP) for h in (c, GQA_GROUP + c))


def _pack_layer_weights(w_in, wa, wx, w_out, w_up, w_gate, w_down):
    d = w_in.shape[0]
    sizes = [ATTN_WIDTH, N_KV_HEADS * HEAD_DIM, N_KV_HEADS * HEAD_DIM, N_IDX_HEADS * IDX_DIM,
             IDX_DIM, N_IDX_HEADS]
    offs = np.cumsum([0] + sizes)
    wq, wk, wv, wqi, wki, wwi = (w_in[:, offs[i]:offs[i + 1]] for i in range(6))
    wrest = w_in[:, offs[6]:]
    rnn_width = wrest.shape[1] // 2
    order = np.asarray(_HEAD_ORDER)
    wq = wq.reshape(d, N_HEADS, HEAD_DIM)[:, order, :].reshape(d, ATTN_WIDTH)
    zpad = jnp.zeros((d, LANES - IDX_DIM - N_IDX_HEADS), w_in.dtype)
    packed = jnp.concatenate([wq, wk, wv, wqi, wki, wwi, zpad, wki, wki, wrest], axis=1).astype(BF16)
    eye = jnp.eye(RNN_BLOCKS, dtype=wa.dtype)
    bd = lambda w: jnp.einsum("nij,nm->nimj", w, eye).reshape(rnn_width, rnn_width).astype(BF16)
    wo_a = w_out[:ATTN_WIDTH].reshape(N_HEADS, HEAD_DIM, -1)[order].reshape(ATTN_WIDTH, -1)
    wo_r = w_out[ATTN_WIDTH:]
    return dict(w_packed=packed, rnn_width=rnn_width, wa_bd=bd(wa), wx_bd=bd(wx),
                wo_a=wo_a.astype(BF16), wo_r=wo_r.astype(BF16), w_up=w_up.astype(BF16),
                w_gate=w_gate.astype(BF16), w_down=w_down.astype(BF16))


def _pad_rows(a, rows):
    return a if a.shape[1] == rows else jnp.pad(a, ((0, 0), (0, rows - a.shape[1]), (0, 0)))


def _round_up(n, m):
    return -(-n // m) * m


def _layer(x, ck, cv, ckidx, h0, rconv0, fconv0, pw, norm_mix, conv_w, conv_b, ba, bx, lam,
           rel_bias, norm_ffn, fconv_w, fconv_b, norm_last, final_norm):
    b, t, d = x.shape
    past = ck.shape[1]
    r = pw["rnn_width"]
    n_tok = b * t
    tm = min(512, n_tok)
    (q, k, v, kb, vb, qi, ki, kk, kw, xr, gr) = _project(
        x.reshape(n_tok, d), norm_mix[None, :], pw["w_packed"], r, tm)
    b3 = lambda a: a.reshape(b, t, a.shape[-1])

    n_keys = past + t
    lp = _round_up(n_keys, KEY_TILE)
    tq = _round_up(t, Q_TILE)
    if past:
        ck2 = ck.reshape(b, past, -1).astype(BF16)
        cv2 = cv.reshape(b, past, -1).astype(BF16)
        cki = ckidx.astype(BF16)
        keys_k = jnp.concatenate([ck2, b3(kb)], axis=1)
        keys_v = jnp.concatenate([cv2, b3(vb)], axis=1)
        keys_i = jnp.concatenate([jnp.concatenate([cki, cki], axis=-1), b3(kk)], axis=1)
    else:
        keys_k, keys_v, keys_i = b3(kb), b3(vb), b3(kk)
    attn = _attention(rel_bias, _pad_rows(b3(q), tq), _pad_rows(b3(qi), tq), _pad_rows(b3(kw), tq),
                      _pad_rows(keys_i, lp), _pad_rows(keys_k, lp), _pad_rows(keys_v, lp),
                      past=past, n_keys=n_keys)[:, :t]

    tr = min(512, t)
    rnn, h_last = _rglru(b3(xr), b3(gr), h0[:, None, :], rconv0, conv_w, conv_b[None, :],
                         pw["wa_bd"], ba[None, :], pw["wx_bd"], bx[None, :], lam[None, :], tr)
    assert t >= RNN_CONV - 1 and t >= FFN_CONV - 1
    rconv_new = b3(xr)[:, t - (RNN_CONV - 1):, :]

    tf = min(256, t)
    y, fconv_new = _ffn(x, attn, rnn, fconv0, pw["wo_a"], pw["wo_r"], norm_ffn[None, :], pw["w_up"],
                        pw["w_gate"], fconv_w, fconv_b[None, :], pw["w_down"], norm_last[None, :],
                        tf, final_norm)
    k_new = k.reshape(b, t, N_KV_HEADS, HEAD_DIM)
    v_new = v.reshape(b, t, N_KV_HEADS, HEAD_DIM)
    return y, (k_new, v_new, b3(ki), h_last[:, 0, :], rconv_new, fconv_new)


def kernel(x_prompt, x_sample, cache_k, cache_v, cache_kidx, state_rglru_h, state_rglru_conv,
           state_ffn_conv, norm_mix, w_in, rglru_conv_w, rglru_conv_b, rglru_wa, rglru_ba,
           rglru_wx, rglru_bx, rglru_lambda, rel_bias, w_out, norm_ffn, w_ffn_up, w_ffn_gate,
           ffn_conv_w, ffn_conv_b, w_ffn_down, norm_final):
    depth = w_in.shape[0]
    bp = x_prompt.shape[0]
    dt = x_prompt.dtype
    xp, xs = x_prompt, x_sample
    outs_p, outs_s = [], []
    for i in range(depth):
        pw = _pack_layer_weights(w_in[i], rglru_wa[i], rglru_wx[i], w_out[i], w_ffn_up[i],
                                 w_ffn_gate[i], w_ffn_down[i])
        r = pw["rnn_width"]
        last = i == depth - 1
        lw = (pw, norm_mix[i], rglru_conv_w[i], rglru_conv_b[i], rglru_ba[i], rglru_bx[i],
              rglru_lambda[i], rel_bias, norm_ffn[i], ffn_conv_w[i], ffn_conv_b[i], norm_final, last)
        xp, st_p = _layer(xp,
                          jnp.zeros((bp, 0, N_KV_HEADS, HEAD_DIM), dt),
                          jnp.zeros((bp, 0, N_KV_HEADS, HEAD_DIM), dt),
                          jnp.zeros((bp, 0, IDX_DIM), dt),
                          jnp.zeros((bp, r), dt),
                          jnp.zeros((bp, RNN_CONV - 1, r), dt),
                          jnp.zeros((bp, FFN_CONV - 1, w_ffn_up.shape[-1]), dt),
                          *lw)
        xs, st_s = _layer(xs, cache_k[i], cache_v[i], cache_kidx[i], state_rglru_h[i],
                          state_rglru_conv[i], state_ffn_conv[i], *lw)
        outs_p.append(st_p)
        outs_s.append(st_s)
    stack = lambda outs, j: jnp.stack([o[j] for o in outs])
    return ((xp, xs) + tuple(stack(outs_p, j) for j in range(6))
            + tuple(stack(outs_s, j) for j in range(6)))
```

```python
import functools
import math

import jax
import jax.numpy as jnp
import numpy as np
from jax import lax
from jax.experimental import pallas as pl
from jax.experimental.pallas import tpu as pltpu

F32 = jnp.float32
BF16 = jnp.bfloat16
I32 = jnp.int32

CHUNK = 64
N_HEADS = 8
HEAD_DIM = 64
N_KV_HEADS = 2
GQA_GROUP = N_HEADS // N_KV_HEADS
ATTN_WIDTH = N_HEADS * HEAD_DIM
N_IDX_HEADS = 8
IDX_DIM = 64
TOPK_MAX = 256
NUM_BUCKETS = 32
MAX_DISTANCE = 128
RNN_BLOCKS = 8
RNN_CONV = 4
RGLRU_C = 8.0
FFN_CONV = 3
EPS = 1e-6

LANES = 128
SUBLANES = 8
VMEM_LIMIT_BYTES = 56 * 1024 * 1024

Q_TILE = 128
KEY_TILE = 256
SEL_CHUNK = 2 * KEY_TILE
SEL_ACCUMULATORS = 8
MASKED = -1e30
INT_MIN = -(2 ** 31)
INT_MAX = 2 ** 31 - 1
F32_MIN_NORMAL_KEY = 2 ** 23

_C_Q = 0
_C_K = _C_Q + ATTN_WIDTH
_C_V = _C_K + LANES
_C_QI = _C_V + LANES
_C_KW = _C_QI + N_IDX_HEADS * IDX_DIM
_C_KK = _C_KW + LANES
_C_XR = _C_KK + LANES


def _rmsnorm(x, g):
    y = x * lax.rsqrt(jnp.mean(x * x, axis=-1, keepdims=True) + EPS)
    return y * g


def _proj_kernel(x_ref, g_ref, w_ref, q_ref, k_ref, v_ref, kb_ref, vb_ref, qi_ref, ki_ref,
                 kk_ref, kw_ref, xr_ref, gr_ref, *, rnn_width):
    h = _rmsnorm(x_ref[...], g_ref[...]).astype(BF16)

    def proj(start, width):
        return jnp.dot(h, w_ref[:, start:start + width], preferred_element_type=F32)

    q_ref[...] = (proj(_C_Q, ATTN_WIDTH) * (HEAD_DIM ** -0.5)).astype(BF16)
    k = proj(_C_K, LANES)
    v = proj(_C_V, LANES)
    k_ref[...] = k
    v_ref[...] = v
    kb_ref[...] = k.astype(BF16)
    vb_ref[...] = v.astype(BF16)
    qi_ref[...] = proj(_C_QI, N_IDX_HEADS * IDX_DIM).astype(BF16)
    kw = proj(_C_KW, LANES)
    kw_ref[...] = kw
    ki_ref[...] = kw[:, :IDX_DIM]
    kk_ref[...] = proj(_C_KK, LANES).astype(BF16)
    xr_ref[...] = proj(_C_XR, rnn_width)
    gr_ref[...] = proj(_C_XR + rnn_width, rnn_width)


def _project(x2d, g, w_packed, rnn_width, tm):
    n, d = x2d.shape
    ncol = w_packed.shape[1]
    row = lambda w: pl.BlockSpec((tm, w), lambda i: (i, 0))
    out_shape = (
        jax.ShapeDtypeStruct((n, ATTN_WIDTH), BF16),
        jax.ShapeDtypeStruct((n, LANES), F32),
        jax.ShapeDtypeStruct((n, LANES), F32),
        jax.ShapeDtypeStruct((n, LANES), BF16),
        jax.ShapeDtypeStruct((n, LANES), BF16),
        jax.ShapeDtypeStruct((n, N_IDX_HEADS * IDX_DIM), BF16),
        jax.ShapeDtypeStruct((n, IDX_DIM), F32),
        jax.ShapeDtypeStruct((n, LANES), BF16),
        jax.ShapeDtypeStruct((n, LANES), F32),
        jax.ShapeDtypeStruct((n, rnn_width), F32),
        jax.ShapeDtypeStruct((n, rnn_width), F32),
    )
    out_specs = (row(ATTN_WIDTH), row(LANES), row(LANES), row(LANES), row(LANES),
                 row(N_IDX_HEADS * IDX_DIM), row(IDX_DIM), row(LANES), row(LANES),
                 row(rnn_width), row(rnn_width))
    return pl.pallas_call(
        functools.partial(_proj_kernel, rnn_width=rnn_width),
        grid=(n // tm,),
        in_specs=[row(d), pl.BlockSpec((1, d), lambda i: (0, 0)),
                  pl.BlockSpec((d, ncol), lambda i: (0, 0))],
        out_specs=out_specs,
        out_shape=out_shape,
        compiler_params=pltpu.CompilerParams(dimension_semantics=("arbitrary",),
                                             vmem_limit_bytes=VMEM_LIMIT_BYTES),
        name="proj",
    )(x2d, g, w_packed)


def _bucket_tables():
    half = NUM_BUCKETS // 2
    max_exact = half // 2
    r = np.arange(Q_TILE, dtype=np.int64)[None, :]
    c = np.arange(KEY_TILE, dtype=np.int64)[:, None]
    tabs = []
    for i in range(3):
        rel = (i - 2) * LANES + c - r
        side = np.where(rel > 0, half, 0)
        n = np.abs(rel)
        nf = np.maximum(n, 1).astype(np.float32)
        ratio = np.log(nf / np.float32(max_exact)) / np.float32(math.log(MAX_DISTANCE / max_exact))
        large = max_exact + (ratio * np.float32(half - max_exact)).astype(np.int32)
        large = np.minimum(large, half - 1)
        tabs.append(side + np.where(n < max_exact, n, large))
    return np.stack(tabs).astype(np.int32)


def _size_classes(total_rows):
    return tuple(range(SEL_CHUNK, total_rows + 1, SEL_CHUNK))


def _skip_subnormal_keys(k, min_normal):
    k = jnp.where((k > 0) & (k < min_normal), min_normal, k)
    return jnp.where((k < 0) & (k > -min_normal), 0, k)


def _key_to_float(k):
    k = _skip_subnormal_keys(k, F32_MIN_NORMAL_KEY)
    bits = jnp.where(k < 0, (-k) | INT_MIN, k)
    return lax.bitcast_convert_type(bits, F32)


def _attn_kernel(rb_ref, bkt_ref, q_ref, qi_ref, kw_ref, kk_ref, kb_ref, vt_ref, o_ref,
                 sc_ref, qit_ref, qt_ref, tbl_ref, s_ref, mx_ref, den_ref, acc_ref,
                 thr_ref, ngt_ref, neq_ref, tie_ref,
                 *, past, n_keys, top_k):
    tq, lt, ct = Q_TILE, KEY_TILE, SEL_CHUNK
    far_bucket = NUM_BUCKETS // 2 - 1
    j = pl.program_id(1)
    q0 = past + j * tq

    @pl.when((pl.program_id(0) == 0) & (j == 0))
    def _():
        for i in range(3):
            bkt = bkt_ref[i]
            for slot, h in enumerate(_HEAD_ORDER):
                t = jnp.zeros((lt, tq), F32)
                for b in range(NUM_BUCKETS):
                    t = jnp.where(bkt == b, rb_ref[b, h], t)
                tbl_ref[i, slot] = t - rb_ref[far_bucket, h]

    lo_rows = lax.broadcasted_iota(I32, (LANES, tq), 0) < HEAD_DIM
    for c in range(N_HEADS // 2):
        qi_t = qi_ref[:, c * LANES:(c + 1) * LANES].astype(F32).T
        q_t = q_ref[:, c * LANES:(c + 1) * LANES].astype(F32).T
        for half in range(2):
            h = 2 * c + half
            keep = lo_rows if half == 0 else jnp.logical_not(lo_rows)
            qit_ref[:, h * tq:(h + 1) * tq] = jnp.where(keep, qi_t, 0.0).astype(BF16)
            qt_ref[:, h * tq:(h + 1) * tq] = jnp.where(keep, q_t, 0.0).astype(BF16)

    wi_t = kw_ref[...].T[IDX_DIM:IDX_DIM + N_IDX_HEADS, :]
    wi_t = (wi_t * (N_IDX_HEADS ** -0.5)) * (IDX_DIM ** -0.5)

    qpos = q0 + lax.broadcasted_iota(I32, (1, tq), 1)
    n_adm = jnp.minimum(((qpos // CHUNK) + 1) * CHUNK, n_keys)
    n_adm_max = jnp.minimum(((q0 + tq - 1) // CHUNK + 1) * CHUNK, n_keys)
    n_kt = (n_adm_max + lt - 1) // lt
    n_ct = (n_adm_max + ct - 1) // ct
    crow = lax.broadcasted_iota(I32, (ct, tq), 0)

    def for_tiles(first, stop, span_fn):
        n = stop - first

        def quads(i, carry):
            span_fn(pl.multiple_of((first + 4 * i) * lt, lt), 4)
            return carry

        lax.fori_loop(0, n // 4, quads, 0)

        @pl.when((n & 2) != 0)
        def _():
            span_fn(pl.multiple_of((first + (n // 4) * 4) * lt, lt), 2)

        @pl.when((n & 1) != 0)
        def _():
            span_fn(pl.multiple_of((stop - 1) * lt, lt), 1)

    def score_span(base, tiles):
        rows = tiles * lt
        s = jnp.dot(kk_ref[pl.ds(base, rows), :], qit_ref[...], preferred_element_type=F32)
        acc = jnp.zeros((rows, tq), F32)
        for h in range(N_IDX_HEADS):
            acc = acc + wi_t[h:h + 1, :] * jnp.maximum(s[:, h * tq:(h + 1) * tq], 0.0)
        key_pos = lax.broadcasted_iota(I32, (rows, tq), 0) + base
        sc_ref[pl.ds(base, rows), :] = jnp.where(key_pos < n_adm, acc, -jnp.inf)

    for_tiles(0, n_kt, score_span)

    def select(rows):
        def blank(kt, carry):
            base = pl.multiple_of(kt * lt, lt)
            sc_ref[pl.ds(base, lt), :] = jnp.full((lt, tq), -jnp.inf, F32)
            return carry

        lax.fori_loop(n_kt, rows // lt, blank, 0)

        def count(cand):
            parts = [None] * SEL_ACCUMULATORS
            for r in range(rows // SUBLANES):
                hit = jnp.where(sc_ref[r * SUBLANES:(r + 1) * SUBLANES, :] >= cand, 1, 0)
                a = parts[r % SEL_ACCUMULATORS]
                parts[r % SEL_ACCUMULATORS] = hit if a is None else a + hit
            total = functools.reduce(lambda a, b: a + b, [a for a in parts if a is not None])
            return jnp.sum(total, axis=0, keepdims=True)

        def step(i, t):
            cand = t + (jnp.int32(1) << (31 - i))
            return jnp.where(count(_key_to_float(cand)) >= top_k, cand, t)

        t = lax.fori_loop(0, 32, step, jnp.full((1, tq), INT_MIN, I32))
        thr = _key_to_float(t)
        n_gt = count(_key_to_float(t + 1))
        thr_ref[...] = thr
        ngt_ref[...] = n_gt
        neq_ref[...] = count(thr) - n_gt

    classes = _size_classes(sc_ref.shape[0])
    for below, rows in zip((0,) + classes[:-1], classes):
        pl.when((n_adm_max > below) & (n_adm_max <= rows))(functools.partial(select, rows))

    thr = thr_ref[...]
    n_gt = ngt_ref[...]
    n_eq = neq_ref[...]

    def count32(pred):
        def body(c, cnt):
            base = pl.multiple_of(c * ct, ct)
            hit = pred(sc_ref[pl.ds(base, ct), :], base).astype(I32)
            return cnt + jnp.sum(hit.reshape(ct // SUBLANES, SUBLANES, tq), axis=0)
        cnt = lax.fori_loop(0, n_ct, body, jnp.zeros((SUBLANES, tq), I32))
        return jnp.sum(cnt, axis=0, keepdims=True)

    want = top_k - n_gt
    finite_thr = thr > -jnp.inf
    need = (n_eq > want) & finite_thr & (qpos < n_keys)
    tie_ref[...] = jnp.where(finite_thr, INT_MAX, -1)

    @pl.when(jnp.max(need.astype(I32)) > 0)
    def _():
        nbits = max(1, int(sc_ref.shape[0]).bit_length())

        def tie_step(i, m):
            cand = m | (jnp.int32(1) << (nbits - 1 - i))
            c = count32(lambda blk, base: (blk == thr) & (crow + base < cand))
            return jnp.where(c < want, cand, m)

        m = lax.fori_loop(0, nbits, tie_step, jnp.zeros((1, tq), I32))
        tie_ref[...] = jnp.where(need, m, tie_ref[...])

    tie_max = tie_ref[...]

    n_far = jnp.clip((q0 - (LANES - 1)) // lt, 0, n_kt)

    def logits_span(base, tiles, near=False):
        rows = tiles * lt
        blk = sc_ref[pl.ds(base, rows), :]
        key_pos = lax.broadcasted_iota(I32, (rows, tq), 0) + base
        sel = (blk > thr) | ((blk == thr) & (key_pos <= tie_max))
        mask = jnp.where(sel, 0.0, MASKED)
        s = jnp.dot(kb_ref[pl.ds(base, rows), :], qt_ref[...], preferred_element_type=F32)
        tops = []
        for h in range(N_HEADS):
            sh = s[:, h * tq:(h + 1) * tq] + mask
            if near:
                sh = sh + tbl_ref[(base - q0) // LANES + 2, h]
            s_ref[pl.ds(base, rows), h * tq:(h + 1) * tq] = sh
            tops.append(jnp.max(sh.reshape(rows // SUBLANES, SUBLANES, tq), axis=0))
        mx_ref[...] = jnp.maximum(mx_ref[...], jnp.concatenate(tops, axis=1))

    def near_tile(kt, carry):
        logits_span(pl.multiple_of(kt * lt, lt), 1, near=True)
        return carry

    mx_ref[...] = jnp.full(mx_ref.shape, MASKED, F32)
    for_tiles(0, n_far, logits_span)
    lax.fori_loop(n_far, n_kt, near_tile, 0)
    m_fin = jnp.max(mx_ref[...], axis=0, keepdims=True)

    acc_ref[...] = jnp.zeros(acc_ref.shape, F32)
    den_ref[...] = jnp.zeros(den_ref.shape, F32)

    def pv_span(base, tiles):
        rows = tiles * lt
        p = jnp.exp(s_ref[pl.ds(base, rows), :] - m_fin)
        kt = base // lt
        values_t = jnp.concatenate([vt_ref[kt + t] for t in range(tiles)], axis=1)
        acc_ref[...] += jnp.dot(values_t, p.astype(BF16), preferred_element_type=F32)
        den_ref[...] += jnp.sum(p.reshape(rows // SUBLANES, SUBLANES, N_HEADS * tq), axis=0)

    for_tiles(0, n_kt, pv_span)
    out_t = acc_ref[...] / jnp.sum(den_ref[...], axis=0, keepdims=True)
    lo_lanes = lax.broadcasted_iota(I32, (tq, LANES), 1) < HEAD_DIM
    for c in range(N_HEADS // 2):
        lo = out_t[:, (2 * c) * tq:(2 * c + 1) * tq].T
        hi = out_t[:, (2 * c + 1) * tq:(2 * c + 2) * tq].T
        o_ref[:, c * LANES:(c + 1) * LANES] = jnp.where(lo_lanes, lo, hi).astype(o_ref.dtype)


def _attention(rel_bias, q, qi, kw, kk, kb, vb, *, past, n_keys):
    b, tq_total, _ = q.shape
    lp = kk.shape[1]
    top_k = min(TOPK_MAX, n_keys // 4)
    assert tq_total % Q_TILE == 0 and lp % KEY_TILE == 0 and past % KEY_TILE == 0
    assert lp >= n_keys and top_k <= KEY_TILE
    n_kt = lp // KEY_TILE
    lc = _round_up(lp, SEL_CHUNK)
    vt = vb.reshape(b, n_kt, KEY_TILE, LANES).swapaxes(2, 3)
    bkt = jnp.asarray(_bucket_tables())
    qspec = lambda w: pl.BlockSpec((None, Q_TILE, w), lambda bi, j: (bi, j, 0))
    kspec = pl.BlockSpec((None, lp, LANES), lambda bi, j: (bi, 0, 0))
    return pl.pallas_call(
        functools.partial(_attn_kernel, past=past, n_keys=n_keys, top_k=top_k),
        grid=(b, tq_total // Q_TILE),
        in_specs=[pl.BlockSpec(memory_space=pltpu.SMEM),
                  pl.BlockSpec((3, KEY_TILE, Q_TILE), lambda bi, j: (0, 0, 0)),
                  qspec(ATTN_WIDTH), qspec(N_IDX_HEADS * IDX_DIM), qspec(LANES),
                  kspec, kspec,
                  pl.BlockSpec((None, n_kt, LANES, KEY_TILE), lambda bi, j: (bi, 0, 0, 0))],
        out_specs=qspec(ATTN_WIDTH),
        out_shape=jax.ShapeDtypeStruct((b, tq_total, ATTN_WIDTH), BF16),
        scratch_shapes=[
            pltpu.VMEM((lc, Q_TILE), F32),
            pltpu.VMEM((LANES, N_IDX_HEADS * Q_TILE), BF16),
            pltpu.VMEM((LANES, N_HEADS * Q_TILE), BF16),
            pltpu.VMEM((3, N_HEADS, KEY_TILE, Q_TILE), F32),
            pltpu.VMEM((lp, N_HEADS * Q_TILE), F32),
            pltpu.VMEM((SUBLANES, N_HEADS * Q_TILE), F32),
            pltpu.VMEM((SUBLANES, N_HEADS * Q_TILE), F32),
            pltpu.VMEM((LANES, N_HEADS * Q_TILE), F32),
            pltpu.VMEM((1, Q_TILE), F32),
            pltpu.VMEM((1, Q_TILE), I32),
            pltpu.VMEM((1, Q_TILE), I32),
            pltpu.VMEM((1, Q_TILE), I32),
        ],
        compiler_params=pltpu.CompilerParams(dimension_semantics=("arbitrary", "arbitrary"),
                                             vmem_limit_bytes=VMEM_LIMIT_BYTES),
        name="attn",
    )(rel_bias, bkt, q, qi, kw, kk, kb, vt)


def _softplus(x):
    return jnp.maximum(x, 0.0) + jnp.log1p(jnp.exp(-jnp.abs(x)))


def _rglru_kernel(xr_ref, gr_ref, h0_ref, c0_ref, cw_ref, cb_ref, wa_ref, ba_ref, wx_ref, bx_ref,
                  lam_ref, rnn_ref, hl_ref, xs_ref, a_ref, u_ref, hs_ref, hc_ref, *, tm):
    t = pl.program_id(1)
    pad = SUBLANES
    hist = RNN_CONV - 1

    @pl.when(t == 0)
    def _():
        xs_ref[pad - hist:pad, :] = c0_ref[...]
        hc_ref[...] = jnp.broadcast_to(h0_ref[...], hc_ref.shape)

    xs_ref[pad:pad + tm, :] = xr_ref[...]
    xc = cb_ref[...] + cw_ref[0:1, :] * xs_ref[pad - hist:pad - hist + tm, :]
    for jj in range(1, RNN_CONV):
        xc = xc + cw_ref[jj:jj + 1, :] * xs_ref[pad - hist + jj:pad - hist + jj + tm, :]
    xs_ref[pad - hist:pad, :] = xs_ref[pad + tm - hist:pad + tm, :]

    xcb = xc.astype(BF16)
    rg = jax.nn.sigmoid(jnp.dot(xcb, wa_ref[...], preferred_element_type=F32) + ba_ref[...])
    ig = jax.nn.sigmoid(jnp.dot(xcb, wx_ref[...], preferred_element_type=F32) + bx_ref[...])
    log_a = (-RGLRU_C * rg) * _softplus(-lam_ref[...])
    a = jnp.exp(log_a)
    u = jnp.sqrt(-jnp.tanh(log_a) * (jnp.exp(2.0 * log_a) + 1.0)) * (ig * xc)

    row = lax.broadcasted_iota(I32, a.shape, 0) % SUBLANES
    for s in (1, 2, 4):
        a_prev = pltpu.roll(a, s, 0)
        u_prev = pltpu.roll(u, s, 0)
        use = row >= s
        u = jnp.where(use, a * u_prev + u, u)
        a = jnp.where(use, a * a_prev, a)
    a_ref[...] = a
    u_ref[...] = u

    def carry_body(g, h):
        base = pl.multiple_of(g * SUBLANES, SUBLANES)
        hg = a_ref[pl.ds(base, SUBLANES), :] * h + u_ref[pl.ds(base, SUBLANES), :]
        hs_ref[pl.ds(base, SUBLANES), :] = hg
        return jnp.broadcast_to(hg[SUBLANES - 1:SUBLANES, :], hg.shape)

    h_end = lax.fori_loop(0, tm // SUBLANES, carry_body, hc_ref[...])
    hc_ref[...] = h_end
    rnn_ref[...] = (hs_ref[...] * jax.nn.gelu(gr_ref[...])).astype(rnn_ref.dtype)

    @pl.when(t == pl.num_programs(1) - 1)
    def _():
        hl_ref[...] = h_end[0:1, :]


def _rglru(xr, gr, h0, conv0, conv_w, conv_b, wa_bd, ba, wx_bd, bx, lam, tm):
    b, t, r = xr.shape
    assert t % tm == 0 and tm % SUBLANES == 0
    tile = pl.BlockSpec((None, tm, r), lambda bi, ti: (bi, ti, 0))
    per_b = lambda rows: pl.BlockSpec((None, rows, r), lambda bi, ti: (bi, 0, 0))
    full = lambda a: pl.BlockSpec(a.shape, lambda bi, ti: (0,) * a.ndim)
    return pl.pallas_call(
        functools.partial(_rglru_kernel, tm=tm),
        grid=(b, t // tm),
        in_specs=[tile, tile, per_b(1), per_b(RNN_CONV - 1), full(conv_w), full(conv_b),
                  full(wa_bd), full(ba), full(wx_bd), full(bx), full(lam)],
        out_specs=(tile, per_b(1)),
        out_shape=(jax.ShapeDtypeStruct((b, t, r), BF16), jax.ShapeDtypeStruct((b, 1, r), F32)),
        scratch_shapes=[pltpu.VMEM((tm + SUBLANES, r), F32), pltpu.VMEM((tm, r), F32),
                        pltpu.VMEM((tm, r), F32), pltpu.VMEM((tm, r), F32),
                        pltpu.VMEM((SUBLANES, r), F32)],
        compiler_params=pltpu.CompilerParams(dimension_semantics=("arbitrary", "arbitrary"),
                                             vmem_limit_bytes=VMEM_LIMIT_BYTES),
        name="rglru",
    )(xr, gr, h0, conv0, conv_w, conv_b, wa_bd, ba, wx_bd, bx, lam)


def _ffn_kernel(x_ref, at_ref, rn_ref, c0_ref, woa_ref, wor_ref, gf_ref, wu_ref, wg_ref, cw_ref,
                cb_ref, wd_ref, gl_ref, y_ref, fc_ref, up_ref, *, tm, final_norm):
    t = pl.program_id(1)
    pad = SUBLANES
    hist = FFN_CONV - 1

    @pl.when(t == 0)
    def _():
        up_ref[pad - hist:pad, :] = c0_ref[...]

    x1 = (x_ref[...]
          + jnp.dot(at_ref[...], woa_ref[...], preferred_element_type=F32)
          + jnp.dot(rn_ref[...], wor_ref[...], preferred_element_type=F32))
    f = _rmsnorm(x1, gf_ref[...]).astype(BF16)
    up_ref[pad:pad + tm, :] = jnp.dot(f, wu_ref[...], preferred_element_type=F32)
    up = cb_ref[...] + cw_ref[0:1, :] * up_ref[pad - hist:pad - hist + tm, :]
    for jj in range(1, FFN_CONV):
        up = up + cw_ref[jj:jj + 1, :] * up_ref[pad - hist + jj:pad - hist + jj + tm, :]
    tail = up_ref[pad + tm - hist:pad + tm, :]
    up_ref[pad - hist:pad, :] = tail
    gate = jnp.dot(f, wg_ref[...], preferred_element_type=F32)
    act = (jax.nn.gelu(up) * gate).astype(BF16)
    x2 = x1 + jnp.dot(act, wd_ref[...], preferred_element_type=F32)
    y_ref[...] = _rmsnorm(x2, gl_ref[...]) if final_norm else x2

    @pl.when(t == pl.num_programs(1) - 1)
    def _():
        fc_ref[...] = tail


def _ffn(x, attn, rnn, conv0, wo_a, wo_r, g_ffn, w_up, w_gate, conv_w, conv_b, w_down, g_last,
         tm, final_norm):
    b, t, d = x.shape
    dff = w_up.shape[1]
    assert t % tm == 0
    tile = lambda w: pl.BlockSpec((None, tm, w), lambda bi, ti: (bi, ti, 0))
    per_b = pl.BlockSpec((None, FFN_CONV - 1, dff), lambda bi, ti: (bi, 0, 0))
    full = lambda a: pl.BlockSpec(a.shape, lambda bi, ti: (0,) * a.ndim,
                                  pipeline_mode=pl.Buffered(1))
    return pl.pallas_call(
        functools.partial(_ffn_kernel, tm=tm, final_norm=final_norm),
        grid=(b, t // tm),
        in_specs=[tile(d), tile(attn.shape[-1]), tile(rnn.shape[-1]), per_b, full(wo_a), full(wo_r),
                  full(g_ffn), full(w_up), full(w_gate), full(conv_w), full(conv_b), full(w_down),
                  full(g_last)],
        out_specs=(tile(d), per_b),
        out_shape=(jax.ShapeDtypeStruct((b, t, d), F32),
                   jax.ShapeDtypeStruct((b, FFN_CONV - 1, dff), F32)),
        scratch_shapes=[pltpu.VMEM((tm + SUBLANES, dff), F32)],
        compiler_params=pltpu.CompilerParams(dimension_semantics=("arbitrary", "arbitrary"),
                                             vmem_limit_bytes=VMEM_LIMIT_BYTES),
        name="ffn",
    )(x, attn, rnn, conv0, wo_a, wo_r, g_ffn, w_up, w_gate, conv_w, conv_b, w_down, g_last)


_HEAD_ORDER = tuple(h for c in range(GQA_GROUP) for h in (c, GQA_GROUP + c))


def _pack_layer_weights(w_in, wa, wx, w_out, w_up, w_gate, w_down):
    d = w_in.shape[0]
    sizes = [ATTN_WIDTH, N_KV_HEADS * HEAD_DIM, N_KV_HEADS * HEAD_DIM, N_IDX_HEADS * IDX_DIM,
             IDX_DIM, N_IDX_HEADS]
    offs = np.cumsum([0] + sizes)
    wq, wk, wv, wqi, wki, wwi = (w_in[:, offs[i]:offs[i + 1]] for i in range(6))
    wrest = w_in[:, offs[6]:]
    rnn_width = wrest.shape[1] // 2
    order = np.asarray(_HEAD_ORDER)
    wq = wq.reshape(d, N_HEADS, HEAD_DIM)[:, order, :].reshape(d, ATTN_WIDTH)
    zpad = jnp.zeros((d, LANES - IDX_DIM - N_IDX_HEADS), w_in.dtype)
    packed = jnp.concatenate([wq, wk, wv, wqi, wki, wwi, zpad, wki, wki, wrest], axis=1).astype(BF16)
    eye = jnp.eye(RNN_BLOCKS, dtype=wa.dtype)
    bd = lambda w: jnp.einsum("nij,nm->nimj", w, eye).reshape(rnn_width, rnn_width).astype(BF16)
    wo_a = w_out[:ATTN_WIDTH].reshape(N_HEADS, HEAD_DIM, -1)[order].reshape(ATTN_WIDTH, -1)
    wo_r = w_out[ATTN_WIDTH:]
    return dict(w_packed=packed, rnn_width=rnn_width, wa_bd=bd(wa), wx_bd=bd(wx),
                wo_a=wo_a.astype(BF16), wo_r=wo_r.astype(BF16), w_up=w_up.astype(BF16),
                w_gate=w_gate.astype(BF16), w_down=w_down.astype(BF16))


def _pad_rows(a, rows):
    return a if a.shape[1] == rows else jnp.pad(a, ((0, 0), (0, rows - a.shape[1]), (0, 0)))


def _round_up(n, m):
    return -(-n // m) * m


def _layer(x, ck, cv, ckidx, h0, rconv0, fconv0, pw, norm_mix, conv_w, conv_b, ba, bx, lam,
           rel_bias, norm_ffn, fconv_w, fconv_b, norm_last, final_norm):
    b, t, d = x.shape
    past = ck.shape[1]
    r = pw["rnn_width"]
    n_tok = b * t
    tm = min(512, n_tok)
    (q, k, v, kb, vb, qi, ki, kk, kw, xr, gr) = _project(
        x.reshape(n_tok, d), norm_mix[None, :], pw["w_packed"], r, tm)
    b3 = lambda a: a.reshape(b, t, a.shape[-1])

    n_keys = past + t
    lp = _round_up(n_keys, KEY_TILE)
    tq = _round_up(t, Q_TILE)
    if past:
        ck2 = ck.reshape(b, past, -1).astype(BF16)
        cv2 = cv.reshape(b, past, -1).astype(BF16)
        cki = ckidx.astype(BF16)
        keys_k = jnp.concatenate([ck2, b3(kb)], axis=1)
        keys_v = jnp.concatenate([cv2, b3(vb)], axis=1)
        keys_i = jnp.concatenate([jnp.concatenate([cki, cki], axis=-1), b3(kk)], axis=1)
    else:
        keys_k, keys_v, keys_i = b3(kb), b3(vb), b3(kk)
    attn = _attention(rel_bias, _pad_rows(b3(q), tq), _pad_rows(b3(qi), tq), _pad_rows(b3(kw), tq),
                      _pad_rows(keys_i, lp), _pad_rows(keys_k, lp), _pad_rows(keys_v, lp),
                      past=past, n_keys=n_keys)[:, :t]

    tr = min(512, t)
    rnn, h_last = _rglru(b3(xr), b3(gr), h0[:, None, :], rconv0, conv_w, conv_b[None, :],
                         pw["wa_bd"], ba[None, :], pw["wx_bd"], bx[None, :], lam[None, :], tr)
    assert t >= RNN_CONV - 1 and t >= FFN_CONV - 1
    rconv_new = b3(xr)[:, t - (RNN_CONV - 1):, :]

    tf = min(256, t)
    y, fconv_new = _ffn(x, attn, rnn, fconv0, pw["wo_a"], pw["wo_r"], norm_ffn[None, :], pw["w_up"],
                        pw["w_gate"], fconv_w, fconv_b[None, :], pw["w_down"], norm_last[None, :],
                        tf, final_norm)
    k_new = k.reshape(b, t, N_KV_HEADS, HEAD_DIM)
    v_new = v.reshape(b, t, N_KV_HEADS, HEAD_DIM)
    return y, (k_new, v_new, b3(ki), h_last[:, 0, :], rconv_new, fconv_new)


def kernel(x_prompt, x_sample, cache_k, cache_v, cache_kidx, state_rglru_h, state_rglru_conv,
           state_ffn_conv, norm_mix, w_in, rglru_conv_w, rglru_conv_b, rglru_wa, rglru_ba,
           rglru_wx, rglru_bx, rglru_lambda, rel_bias, w_out, norm_ffn, w_ffn_up, w_ffn_gate,
           ffn_conv_w, ffn_conv_b, w_ffn_down, norm_final):
    depth = w_in.shape[0]
    bp = x_prompt.shape[0]
    dt = x_prompt.dtype
    xp, xs = x_prompt, x_sample
    outs_p, outs_s = [], []
    for i in range(depth):
        pw = _pack_layer_weights(w_in[i], rglru_wa[i], rglru_wx[i], w_out[i], w_ffn_up[i],
                                 w_ffn_gate[i], w_ffn_down[i])
        r = pw["rnn_width"]
        last = i == depth - 1
        lw = (pw, norm_mix[i], rglru_conv_w[i], rglru_conv_b[i], rglru_ba[i], rglru_bx[i],
              rglru_lambda[i], rel_bias, norm_ffn[i], ffn_conv_w[i], ffn_conv_b[i], norm_final, last)
        xp, st_p = _layer(xp,
                          jnp.zeros((bp, 0, N_KV_HEADS, HEAD_DIM), dt),
                          jnp.zeros((bp, 0, N_KV_HEADS, HEAD_DIM), dt),
                          jnp.zeros((bp, 0, IDX_DIM), dt),
                          jnp.zeros((bp, r), dt),
                          jnp.zeros((bp, RNN_CONV - 1, r), dt),
                          jnp.zeros((bp, FFN_CONV - 1, w_ffn_up.shape[-1]), dt),
                          *lw)
        xs, st_s = _layer(xs, cache_k[i], cache_v[i], cache_kidx[i], state_rglru_h[i],
                          state_rglru_conv[i], state_ffn_conv[i], *lw)
        outs_p.append(st_p)
        outs_s.append(st_s)
    stack = lambda outs, j: jnp.stack([o[j] for o in outs])
    return ((xp, xs) + tuple(stack(outs_p, j) for j in range(6))
            + tuple(stack(outs_s, j) for j in range(6)))
```

```python
import functools
import math

import jax
import jax.numpy as jnp
import numpy as np
from jax import lax
from jax.experimental import pallas as pl
from jax.experimental.pallas import tpu as pltpu

F32 = jnp.float32
BF16 = jnp.bfloat16
I32 = jnp.int32

CHUNK = 64
N_HEADS = 8
HEAD_DIM = 64
N_KV_HEADS = 2
GQA_GROUP = N_HEADS // N_KV_HEADS
ATTN_WIDTH = N_HEADS * HEAD_DIM
N_IDX_HEADS = 8
IDX_DIM = 64
TOPK_MAX = 256
NUM_BUCKETS = 32
MAX_DISTANCE = 128
RNN_BLOCKS = 8
RNN_CONV = 4
RGLRU_C = 8.0
FFN_CONV = 3
EPS = 1e-6

LANES = 128
SUBLANES = 8
VMEM_LIMIT_BYTES = 56 * 1024 * 1024

Q_TILE = 128
KEY_TILE = 256
SEL_CHUNK = 2 * KEY_TILE
SEL_ACCUMULATORS = 8
MASKED = -1e30
INT_MIN = -(2 ** 31)
INT_MAX = 2 ** 31 - 1
F32_MIN_NORMAL_KEY = 2 ** 23
F32_INF_KEY = 0x7F800000
SEL_MARGIN = 2 ** 16

_C_Q = 0
_C_K = _C_Q + ATTN_WIDTH
_C_V = _C_K + LANES
_C_QI = _C_V + LANES
_C_KW = _C_QI + N_IDX_HEADS * IDX_DIM
_C_KK = _C_KW + LANES
_C_XR = _C_KK + LANES


def _rmsnorm(x, g):
    y = x * lax.rsqrt(jnp.mean(x * x, axis=-1, keepdims=True) + EPS)
    return y * g


def _proj_kernel(x_ref, g_ref, w_ref, q_ref, k_ref, v_ref, kb_ref, vb_ref, qi_ref, ki_ref,
                 kk_ref, kw_ref, xr_ref, gr_ref, *, rnn_width):
    h = _rmsnorm(x_ref[...], g_ref[...]).astype(BF16)

    def proj(start, width):
        return jnp.dot(h, w_ref[:, start:start + width], preferred_element_type=F32)

    q_ref[...] = (proj(_C_Q, ATTN_WIDTH) * (HEAD_DIM ** -0.5)).astype(BF16)
    k = proj(_C_K, LANES)
    v = proj(_C_V, LANES)
    k_ref[...] = k
    v_ref[...] = v
    kb_ref[...] = k.astype(BF16)
    vb_ref[...] = v.astype(BF16)
    qi_ref[...] = proj(_C_QI, N_IDX_HEADS * IDX_DIM).astype(BF16)
    kw = proj(_C_KW, LANES)
    kw_ref[...] = kw
    ki_ref[...] = kw[:, :IDX_DIM]
    kk_ref[...] = proj(_C_KK, LANES).astype(BF16)
    xr_ref[...] = proj(_C_XR, rnn_width)
    gr_ref[...] = proj(_C_XR + rnn_width, rnn_width)


def _project(x2d, g, w_packed, rnn_width, tm):
    n, d = x2d.shape
    ncol = w_packed.shape[1]
    row = lambda w: pl.BlockSpec((tm, w), lambda i: (i, 0))
    out_shape = (
        jax.ShapeDtypeStruct((n, ATTN_WIDTH), BF16),
        jax.ShapeDtypeStruct((n, LANES), F32),
        jax.ShapeDtypeStruct((n, LANES), F32),
        jax.ShapeDtypeStruct((n, LANES), BF16),
        jax.ShapeDtypeStruct((n, LANES), BF16),
        jax.ShapeDtypeStruct((n, N_IDX_HEADS * IDX_DIM), BF16),
        jax.ShapeDtypeStruct((n, IDX_DIM), F32),
        jax.ShapeDtypeStruct((n, LANES), BF16),
        jax.ShapeDtypeStruct((n, LANES), F32),
        jax.ShapeDtypeStruct((n, rnn_width), F32),
        jax.ShapeDtypeStruct((n, rnn_width), F32),
    )
    out_specs = (row(ATTN_WIDTH), row(LANES), row(LANES), row(LANES), row(LANES),
                 row(N_IDX_HEADS * IDX_DIM), row(IDX_DIM), row(LANES), row(LANES),
                 row(rnn_width), row(rnn_width))
    return pl.pallas_call(
        functools.partial(_proj_kernel, rnn_width=rnn_width),
        grid=(n // tm,),
        in_specs=[row(d), pl.BlockSpec((1, d), lambda i: (0, 0)),
                  pl.BlockSpec((d, ncol), lambda i: (0, 0))],
        out_specs=out_specs,
        out_shape=out_shape,
        compiler_params=pltpu.CompilerParams(dimension_semantics=("arbitrary",),
                                             vmem_limit_bytes=VMEM_LIMIT_BYTES),
        name="proj",
    )(x2d, g, w_packed)


def _bucket_tables():
    half = NUM_BUCKETS // 2
    max_exact = half // 2
    r = np.arange(Q_TILE, dtype=np.int64)[None, :]
    c = np.arange(KEY_TILE, dtype=np.int64)[:, None]
    tabs = []
    for i in range(3):
        rel = (i - 2) * LANES + c - r
        side = np.where(rel > 0, half, 0)
        n = np.abs(rel)
        nf = np.maximum(n, 1).astype(np.float32)
        ratio = np.log(nf / np.float32(max_exact)) / np.float32(math.log(MAX_DISTANCE / max_exact))
        large = max_exact + (ratio * np.float32(half - max_exact)).astype(np.int32)
        large = np.minimum(large, half - 1)
        tabs.append(side + np.where(n < max_exact, n, large))
    return np.stack(tabs).astype(np.int32)


def _size_classes(total_rows):
    return tuple(range(SEL_CHUNK, total_rows + 1, SEL_CHUNK))


def _skip_subnormal_keys(k, min_normal):
    k = jnp.where((k > 0) & (k < min_normal), min_normal, k)
    return jnp.where((k < 0) & (k > -min_normal), 0, k)


def _float_to_key(x):
    bits = lax.bitcast_convert_type(x, I32)
    return jnp.where(bits < 0, INT_MIN - bits, bits)


def _key_to_float(k):
    k = _skip_subnormal_keys(k, F32_MIN_NORMAL_KEY)
    bits = jnp.where(k < 0, (-k) | INT_MIN, k)
    return lax.bitcast_convert_type(bits, F32)


def _attn_kernel(rb_ref, bkt_ref, q_ref, qi_ref, kw_ref, kk_ref, kb_ref, vt_ref, o_ref,
                 sc_ref, qit_ref, qt_ref, tbl_ref, s_ref, mx_ref, den_ref, acc_ref,
                 thr_ref, ngt_ref, neq_ref, tie_ref,
                 *, past, n_keys, top_k):
    tq, lt, ct = Q_TILE, KEY_TILE, SEL_CHUNK
    far_bucket = NUM_BUCKETS // 2 - 1
    j = pl.program_id(1)
    q0 = past + j * tq

    @pl.when((pl.program_id(0) == 0) & (j == 0))
    def _():
        for i in range(3):
            bkt = bkt_ref[i]
            for slot, h in enumerate(_HEAD_ORDER):
                t = jnp.zeros((lt, tq), F32)
                for b in range(NUM_BUCKETS):
                    t = jnp.where(bkt == b, rb_ref[b, h], t)
                tbl_ref[i, slot] = t - rb_ref[far_bucket, h]

    lo_rows = lax.broadcasted_iota(I32, (LANES, tq), 0) < HEAD_DIM
    for c in range(N_HEADS // 2):
        qi_t = qi_ref[:, c * LANES:(c + 1) * LANES].astype(F32).T
        q_t = q_ref[:, c * LANES:(c + 1) * LANES].astype(F32).T
        for half in range(2):
            h = 2 * c + half
            keep = lo_rows if half == 0 else jnp.logical_not(lo_rows)
            qit_ref[:, h * tq:(h + 1) * tq] = jnp.where(keep, qi_t, 0.0).astype(BF16)
            qt_ref[:, h * tq:(h + 1) * tq] = jnp.where(keep, q_t, 0.0).astype(BF16)

    wi_t = kw_ref[...].T[IDX_DIM:IDX_DIM + N_IDX_HEADS, :]
    wi_t = (wi_t * (N_IDX_HEADS ** -0.5)) * (IDX_DIM ** -0.5)

    qpos = q0 + lax.broadcasted_iota(I32, (1, tq), 1)
    n_adm = jnp.minimum(((qpos // CHUNK) + 1) * CHUNK, n_keys)
    n_adm_max = jnp.minimum(((q0 + tq - 1) // CHUNK + 1) * CHUNK, n_keys)
    n_kt = (n_adm_max + lt - 1) // lt
    n_ct = (n_adm_max + ct - 1) // ct
    crow = lax.broadcasted_iota(I32, (ct, tq), 0)

    def for_tiles(first, stop, span_fn):
        n = stop - first

        def quads(i, carry):
            span_fn(pl.multiple_of((first + 4 * i) * lt, lt), 4)
            return carry

        lax.fori_loop(0, n // 4, quads, 0)

        @pl.when((n & 2) != 0)
        def _():
            span_fn(pl.multiple_of((first + (n // 4) * 4) * lt, lt), 2)

        @pl.when((n & 1) != 0)
        def _():
            span_fn(pl.multiple_of((stop - 1) * lt, lt), 1)

    def score_span(base, tiles):
        rows = tiles * lt
        s = jnp.dot(kk_ref[pl.ds(base, rows), :], qit_ref[...], preferred_element_type=F32)
        acc = jnp.zeros((rows, tq), F32)
        for h in range(N_IDX_HEADS):
            acc = acc + wi_t[h:h + 1, :] * jnp.maximum(s[:, h * tq:(h + 1) * tq], 0.0)
        key_pos = lax.broadcasted_iota(I32, (rows, tq), 0) + base
        sc_ref[pl.ds(base, rows), :] = jnp.where(key_pos < n_adm, acc, -jnp.inf)

    for_tiles(0, n_kt, score_span)

    def select(rows):
        def blank(kt, carry):
            base = pl.multiple_of(kt * lt, lt)
            sc_ref[pl.ds(base, lt), :] = jnp.full((lt, tq), -jnp.inf, F32)
            return carry

        lax.fori_loop(n_kt, rows // lt, blank, 0)

        def count(cand):
            parts = [None] * SEL_ACCUMULATORS
            for r in range(rows // SUBLANES):
                hit = jnp.where(sc_ref[r * SUBLANES:(r + 1) * SUBLANES, :] >= cand, 1, 0)
                a = parts[r % SEL_ACCUMULATORS]
                parts[r % SEL_ACCUMULATORS] = hit if a is None else a + hit
            total = functools.reduce(lambda a, b: a + b, [a for a in parts if a is not None])
            return jnp.sum(total, axis=0, keepdims=True)

        tile_max = sc_ref[0:lt, :]
        for r in range(1, rows // lt):
            tile_max = jnp.maximum(tile_max, sc_ref[r * lt:(r + 1) * lt, :])
        k_low = _float_to_key(jnp.min(tile_max, axis=0, keepdims=True))
        k_top = _float_to_key(jnp.max(tile_max, axis=0, keepdims=True))
        lo = jnp.maximum(k_low - SEL_MARGIN, -F32_INF_KEY)
        lo = jnp.where(count(_key_to_float(lo)) >= top_k, lo, -F32_INF_KEY)
        hi = jnp.minimum(k_top + SEL_MARGIN, F32_INF_KEY) + 1
        n_steps = jnp.max(32 - lax.clz(hi - lo - 1))

        def step(i, carry):
            lo, hi = carry
            mid = lo + lax.shift_right_logical(hi - lo, 1)
            ok = count(_key_to_float(mid)) >= top_k
            return jnp.where(ok, mid, lo), jnp.where(ok, hi, mid)

        t, _ = lax.fori_loop(0, n_steps, step, (lo, hi))
        thr = _key_to_float(t)
        n_gt = count(_key_to_float(t + 1))
        thr_ref[...] = thr
        ngt_ref[...] = n_gt
        neq_ref[...] = count(thr) - n_gt

    classes = _size_classes(sc_ref.shape[0])
    for below, rows in zip((0,) + classes[:-1], classes):
        pl.when((n_adm_max > below) & (n_adm_max <= rows))(functools.partial(select, rows))

    thr = thr_ref[...]
    n_gt = ngt_ref[...]
    n_eq = neq_ref[...]

    def count32(pred):
        def body(c, cnt):
            base = pl.multiple_of(c * ct, ct)
            hit = pred(sc_ref[pl.ds(base, ct), :], base).astype(I32)
            return cnt + jnp.sum(hit.reshape(ct // SUBLANES, SUBLANES, tq), axis=0)
        cnt = lax.fori_loop(0, n_ct, body, jnp.zeros((SUBLANES, tq), I32))
        return jnp.sum(cnt, axis=0, keepdims=True)

    want = top_k - n_gt
    finite_thr = thr > -jnp.inf
    need = (n_eq > want) & finite_thr & (qpos < n_keys)
    tie_ref[...] = jnp.where(finite_thr, INT_MAX, -1)

    @pl.when(jnp.max(need.astype(I32)) > 0)
    def _():
        nbits = max(1, int(sc_ref.shape[0]).bit_length())

        def tie_step(i, m):
            cand = m | (jnp.int32(1) << (nbits - 1 - i))
            c = count32(lambda blk, base: (blk == thr) & (crow + base < cand))
            return jnp.where(c < want, cand, m)

        m = lax.fori_loop(0, nbits, tie_step, jnp.zeros((1, tq), I32))
        tie_ref[...] = jnp.where(need, m, tie_ref[...])

    tie_max = tie_ref[...]

    n_far = jnp.clip((q0 - (LANES - 1)) // lt, 0, n_kt)

    def logits_span(base, tiles, near=False):
        rows = tiles * lt
        blk = sc_ref[pl.ds(base, rows), :]
        key_pos = lax.broadcasted_iota(I32, (rows, tq), 0) + base
        sel = (blk > thr) | ((blk == thr) & (key_pos <= tie_max))
        mask = jnp.where(sel, 0.0, MASKED)
        s = jnp.dot(kb_ref[pl.ds(base, rows), :], qt_ref[...], preferred_element_type=F32)
        tops = []
        for h in range(N_HEADS):
            sh = s[:, h * tq:(h + 1) * tq] + mask
            if near:
                sh = sh + tbl_ref[(base - q0) // LANES + 2, h]
            s_ref[pl.ds(base, rows), h * tq:(h + 1) * tq] = sh
            tops.append(jnp.max(sh.reshape(rows // SUBLANES, SUBLANES, tq), axis=0))
        mx_ref[...] = jnp.maximum(mx_ref[...], jnp.concatenate(tops, axis=1))

    def near_tile(kt, carry):
        logits_span(pl.multiple_of(kt * lt, lt), 1, near=True)
        return carry

    mx_ref[...] = jnp.full(mx_ref.shape, MASKED, F32)
    for_tiles(0, n_far, logits_span)
    lax.fori_loop(n_far, n_kt, near_tile, 0)
    m_fin = jnp.max(mx_ref[...], axis=0, keepdims=True)

    acc_ref[...] = jnp.zeros(acc_ref.shape, F32)
    den_ref[...] = jnp.zeros(den_ref.shape, F32)

    def pv_span(base, tiles):
        rows = tiles * lt
        p = jnp.exp(s_ref[pl.ds(base, rows), :] - m_fin)
        kt = base // lt
        values_t = jnp.concatenate([vt_ref[kt + t] for t in range(tiles)], axis=1)
        acc_ref[...] += jnp.dot(values_t, p.astype(BF16), preferred_element_type=F32)
        den_ref[...] += jnp.sum(p.reshape(rows // SUBLANES, SUBLANES, N_HEADS * tq), axis=0)

    for_tiles(0, n_kt, pv_span)
    out_t = acc_ref[...] / jnp.sum(den_ref[...], axis=0, keepdims=True)
    lo_lanes = lax.broadcasted_iota(I32, (tq, LANES), 1) < HEAD_DIM
    for c in range(N_HEADS // 2):
        lo = out_t[:, (2 * c) * tq:(2 * c + 1) * tq].T
        hi = out_t[:, (2 * c + 1) * tq:(2 * c + 2) * tq].T
        o_ref[:, c * LANES:(c + 1) * LANES] = jnp.where(lo_lanes, lo, hi).astype(o_ref.dtype)


def _attention(rel_bias, q, qi, kw, kk, kb, vb, *, past, n_keys):
    b, tq_total, _ = q.shape
    lp = kk.shape[1]
    top_k = min(TOPK_MAX, n_keys // 4)
    assert tq_total % Q_TILE == 0 and lp % KEY_TILE == 0 and past % KEY_TILE == 0
    assert lp >= n_keys and top_k <= KEY_TILE
    n_kt = lp // KEY_TILE
    lc = _round_up(lp, SEL_CHUNK)
    vt = vb.reshape(b, n_kt, KEY_TILE, LANES).swapaxes(2, 3)
    bkt = jnp.asarray(_bucket_tables())
    qspec = lambda w: pl.BlockSpec((None, Q_TILE, w), lambda bi, j: (bi, j, 0))
    kspec = pl.BlockSpec((None, lp, LANES), lambda bi, j: (bi, 0, 0))
    return pl.pallas_call(
        functools.partial(_attn_kernel, past=past, n_keys=n_keys, top_k=top_k),
        grid=(b, tq_total // Q_TILE),
        in_specs=[pl.BlockSpec(memory_space=pltpu.SMEM),
                  pl.BlockSpec((3, KEY_TILE, Q_TILE), lambda bi, j: (0, 0, 0)),
                  qspec(ATTN_WIDTH), qspec(N_IDX_HEADS * IDX_DIM), qspec(LANES),
                  kspec, kspec,
                  pl.BlockSpec((None, n_kt, LANES, KEY_TILE), lambda bi, j: (bi, 0, 0, 0))],
        out_specs=qspec(ATTN_WIDTH),
        out_shape=jax.ShapeDtypeStruct((b, tq_total, ATTN_WIDTH), BF16),
        scratch_shapes=[
            pltpu.VMEM((lc, Q_TILE), F32),
            pltpu.VMEM((LANES, N_IDX_HEADS * Q_TILE), BF16),
            pltpu.VMEM((LANES, N_HEADS * Q_TILE), BF16),
            pltpu.VMEM((3, N_HEADS, KEY_TILE, Q_TILE), F32),
            pltpu.VMEM((lp, N_HEADS * Q_TILE), F32),
            pltpu.VMEM((SUBLANES, N_HEADS * Q_TILE), F32),
            pltpu.VMEM((SUBLANES, N_HEADS * Q_TILE), F32),
            pltpu.VMEM((LANES, N_HEADS * Q_TILE), F32),
            pltpu.VMEM((1, Q_TILE), F32),
            pltpu.VMEM((1, Q_TILE), I32),
            pltpu.VMEM((1, Q_TILE), I32),
            pltpu.VMEM((1, Q_TILE), I32),
        ],
        compiler_params=pltpu.CompilerParams(dimension_semantics=("arbitrary", "arbitrary"),
                                             vmem_limit_bytes=VMEM_LIMIT_BYTES),
        name="attn",
    )(rel_bias, bkt, q, qi, kw, kk, kb, vt)


def _softplus(x):
    return jnp.maximum(x, 0.0) + jnp.log1p(jnp.exp(-jnp.abs(x)))


def _rglru_kernel(xr_ref, gr_ref, h0_ref, c0_ref, cw_ref, cb_ref, wa_ref, ba_ref, wx_ref, bx_ref,
                  lam_ref, rnn_ref, hl_ref, xs_ref, a_ref, u_ref, hs_ref, hc_ref, *, tm):
    t = pl.program_id(1)
    pad = SUBLANES
    hist = RNN_CONV - 1

    @pl.when(t == 0)
    def _():
        xs_ref[pad - hist:pad, :] = c0_ref[...]
        hc_ref[...] = jnp.broadcast_to(h0_ref[...], hc_ref.shape)

    xs_ref[pad:pad + tm, :] = xr_ref[...]
    xc = cb_ref[...] + cw_ref[0:1, :] * xs_ref[pad - hist:pad - hist + tm, :]
    for jj in range(1, RNN_CONV):
        xc = xc + cw_ref[jj:jj + 1, :] * xs_ref[pad - hist + jj:pad - hist + jj + tm, :]
    xs_ref[pad - hist:pad, :] = xs_ref[pad + tm - hist:pad + tm, :]

    xcb = xc.astype(BF16)
    rg = jax.nn.sigmoid(jnp.dot(xcb, wa_ref[...], preferred_element_type=F32) + ba_ref[...])
    ig = jax.nn.sigmoid(jnp.dot(xcb, wx_ref[...], preferred_element_type=F32) + bx_ref[...])
    log_a = (-RGLRU_C * rg) * _softplus(-lam_ref[...])
    a = jnp.exp(log_a)
    u = jnp.sqrt(-jnp.tanh(log_a) * (a * a + 1.0)) * (ig * xc)

    groups = (tm // SUBLANES, SUBLANES, a.shape[-1])
    a = a.reshape(groups)
    u = u.reshape(groups)
    row = lax.broadcasted_iota(I32, groups, 1)
    for s in (1, 2, 4):
        a_prev = pltpu.roll(a, s, 1)
        u_prev = pltpu.roll(u, s, 1)
        use = row >= s
        u = jnp.where(use, a * u_prev + u, u)
        a = jnp.where(use, a * a_prev, a)
    a_ref[...] = a.reshape(tm, -1)
    u_ref[...] = u.reshape(tm, -1)

    def carry_body(g, h):
        base = pl.multiple_of(g * SUBLANES, SUBLANES)
        hg = a_ref[pl.ds(base, SUBLANES), :] * h + u_ref[pl.ds(base, SUBLANES), :]
        hs_ref[pl.ds(base, SUBLANES), :] = hg
        return jnp.broadcast_to(hg[SUBLANES - 1:SUBLANES, :], hg.shape)

    h_end = lax.fori_loop(0, tm // SUBLANES, carry_body, hc_ref[...])
    hc_ref[...] = h_end
    rnn_ref[...] = (hs_ref[...] * jax.nn.gelu(gr_ref[...])).astype(rnn_ref.dtype)

    @pl.when(t == pl.num_programs(1) - 1)
    def _():
        hl_ref[...] = h_end[0:1, :]


def _rglru(xr, gr, h0, conv0, conv_w, conv_b, wa_bd, ba, wx_bd, bx, lam, tm):
    b, t, r = xr.shape
    assert t % tm == 0 and tm % SUBLANES == 0
    tile = pl.BlockSpec((None, tm, r), lambda bi, ti: (bi, ti, 0))
    per_b = lambda rows: pl.BlockSpec((None, rows, r), lambda bi, ti: (bi, 0, 0))
    full = lambda a: pl.BlockSpec(a.shape, lambda bi, ti: (0,) * a.ndim)
    return pl.pallas_call(
        functools.partial(_rglru_kernel, tm=tm),
        grid=(b, t // tm),
        in_specs=[tile, tile, per_b(1), per_b(RNN_CONV - 1), full(conv_w), full(conv_b),
                  full(wa_bd), full(ba), full(wx_bd), full(bx), full(lam)],
        out_specs=(tile, per_b(1)),
        out_shape=(jax.ShapeDtypeStruct((b, t, r), BF16), jax.ShapeDtypeStruct((b, 1, r), F32)),
        scratch_shapes=[pltpu.VMEM((tm + SUBLANES, r), F32), pltpu.VMEM((tm, r), F32),
                        pltpu.VMEM((tm, r), F32), pltpu.VMEM((tm, r), F32),
                        pltpu.VMEM((SUBLANES, r), F32)],
        compiler_params=pltpu.CompilerParams(dimension_semantics=("arbitrary", "arbitrary"),
                                             vmem_limit_bytes=VMEM_LIMIT_BYTES),
        name="rglru",
    )(xr, gr, h0, conv0, conv_w, conv_b, wa_bd, ba, wx_bd, bx, lam)


def _ffn_kernel(x_ref, at_ref, rn_ref, c0_ref, woa_ref, wor_ref, gf_ref, wu_ref, wg_ref, cw_ref,
                cb_ref, wd_ref, gl_ref, y_ref, fc_ref, up_ref, *, tm, final_norm):
    t = pl.program_id(1)
    pad = SUBLANES
    hist = FFN_CONV - 1

    @pl.when(t == 0)
    def _():
        up_ref[pad - hist:pad, :] = c0_ref[...]

    x1 = (x_ref[...]
          + jnp.dot(at_ref[...], woa_ref[...], preferred_element_type=F32)
          + jnp.dot(rn_ref[...], wor_ref[...], preferred_element_type=F32))
    f = _rmsnorm(x1, gf_ref[...]).astype(BF16)
    up_ref[pad:pad + tm, :] = jnp.dot(f, wu_ref[...], preferred_element_type=F32)
    up = cb_ref[...] + cw_ref[0:1, :] * up_ref[pad - hist:pad - hist + tm, :]
    for jj in range(1, FFN_CONV):
        up = up + cw_ref[jj:jj + 1, :] * up_ref[pad - hist + jj:pad - hist + jj + tm, :]
    tail = up_ref[pad + tm - hist:pad + tm, :]
    up_ref[pad - hist:pad, :] = tail
    gate = jnp.dot(f, wg_ref[...], preferred_element_type=F32)
    act = (jax.nn.gelu(up) * gate).astype(BF16)
    x2 = x1 + jnp.dot(act, wd_ref[...], preferred_element_type=F32)
    y_ref[...] = _rmsnorm(x2, gl_ref[...]) if final_norm else x2

    @pl.when(t == pl.num_programs(1) - 1)
    def _():
        fc_ref[...] = tail


def _ffn(x, attn, rnn, conv0, wo_a, wo_r, g_ffn, w_up, w_gate, conv_w, conv_b, w_down, g_last,
         tm, final_norm):
    b, t, d = x.shape
    dff = w_up.shape[1]
    assert t % tm == 0
    tile = lambda w: pl.BlockSpec((None, tm, w), lambda bi, ti: (bi, ti, 0))
    per_b = pl.BlockSpec((None, FFN_CONV - 1, dff), lambda bi, ti: (bi, 0, 0))
    full = lambda a: pl.BlockSpec(a.shape, lambda bi, ti: (0,) * a.ndim,
                                  pipeline_mode=pl.Buffered(1))
    return pl.pallas_call(
        functools.partial(_ffn_kernel, tm=tm, final_norm=final_norm),
        grid=(b, t // tm),
        in_specs=[tile(d), tile(attn.shape[-1]), tile(rnn.shape[-1]), per_b, full(wo_a), full(wo_r),
                  full(g_ffn), full(w_up), full(w_gate), full(conv_w), full(conv_b), full(w_down),
                  full(g_last)],
        out_specs=(tile(d), per_b),
        out_shape=(jax.ShapeDtypeStruct((b, t, d), F32),
                   jax.ShapeDtypeStruct((b, FFN_CONV - 1, dff), F32)),
        scratch_shapes=[pltpu.VMEM((tm + SUBLANES, dff), F32)],
        compiler_params=pltpu.CompilerParams(dimension_semantics=("arbitrary", "arbitrary"),
                                             vmem_limit_bytes=VMEM_LIMIT_BYTES),
        name="ffn",
    )(x, attn, rnn, conv0, wo_a, wo_r, g_ffn, w_up, w_gate, conv_w, conv_b, w_down, g_last)


_HEAD_ORDER = tuple(h for c in range(GQA_GROUP) for h in (c, GQA_GROUP + c))


def _pack_layer_weights(w_in, wa, wx, w_out, w_up, w_gate, w_down):
    d = w_in.shape[0]
    sizes = [ATTN_WIDTH, N_KV_HEADS * HEAD_DIM, N_KV_HEADS * HEAD_DIM, N_IDX_HEADS * IDX_DIM,
             IDX_DIM, N_IDX_HEADS]
    offs = np.cumsum([0] + sizes)
    wq, wk, wv, wqi, wki, wwi = (w_in[:, offs[i]:offs[i + 1]] for i in range(6))
    wrest = w_in[:, offs[6]:]
    rnn_width = wrest.shape[1] // 2
    order = np.asarray(_HEAD_ORDER)
    wq = wq.reshape(d, N_HEADS, HEAD_DIM)[:, order, :].reshape(d, ATTN_WIDTH)
    zpad = jnp.zeros((d, LANES - IDX_DIM - N_IDX_HEADS), w_in.dtype)
    packed = jnp.concatenate([wq, wk, wv, wqi, wki, wwi, zpad, wki, wki, wrest], axis=1).astype(BF16)
    eye = jnp.eye(RNN_BLOCKS, dtype=wa.dtype)
    bd = lambda w: jnp.einsum("nij,nm->nimj", w, eye).reshape(rnn_width, rnn_width).astype(BF16)
    wo_a = w_out[:ATTN_WIDTH].reshape(N_HEADS, HEAD_DIM, -1)[order].reshape(ATTN_WIDTH, -1)
    wo_r = w_out[ATTN_WIDTH:]
    return dict(w_packed=packed, rnn_width=rnn_width, wa_bd=bd(wa), wx_bd=bd(wx),
                wo_a=wo_a.astype(BF16), wo_r=wo_r.astype(BF16), w_up=w_up.astype(BF16),
                w_gate=w_gate.astype(BF16), w_down=w_down.astype(BF16))


def _pad_rows(a, rows):
    return a if a.shape[1] == rows else jnp.pad(a, ((0, 0), (0, rows - a.shape[1]), (0, 0)))


def _round_up(n, m):
    return -(-n // m) * m


def _layer(x, ck, cv, ckidx, h0, rconv0, fconv0, pw, norm_mix, conv_w, conv_b, ba, bx, lam,
           rel_bias, norm_ffn, fconv_w, fconv_b, norm_last, final_norm):
    b, t, d = x.shape
    past = ck.shape[1]
    r = pw["rnn_width"]
    n_tok = b * t
    tm = min(512, n_tok)
    (q, k, v, kb, vb, qi, ki, kk, kw, xr, gr) = _project(
        x.reshape(n_tok, d), norm_mix[None, :], pw["w_packed"], r, tm)
    b3 = lambda a: a.reshape(b, t, a.shape[-1])

    n_keys = past + t
    lp = _round_up(n_keys, KEY_TILE)
    tq = _round_up(t, Q_TILE)
    if past:
        ck2 = ck.reshape(b, past, -1).astype(BF16)
        cv2 = cv.reshape(b, past, -1).astype(BF16)
        cki = ckidx.astype(BF16)
        keys_k = jnp.concatenate([ck2, b3(kb)], axis=1)
        keys_v = jnp.concatenate([cv2, b3(vb)], axis=1)
        keys_i = jnp.concatenate([jnp.concatenate([cki, cki], axis=-1), b3(kk)], axis=1)
    else:
        keys_k, keys_v, keys_i = b3(kb), b3(vb), b3(kk)
    attn = _attention(rel_bias, _pad_rows(b3(q), tq), _pad_rows(b3(qi), tq), _pad_rows(b3(kw), tq),
                      _pad_rows(keys_i, lp), _pad_rows(keys_k, lp), _pad_rows(keys_v, lp),
                      past=past, n_keys=n_keys)[:, :t]

    tr = min(512, t)
    rnn, h_last = _rglru(b3(xr), b3(gr), h0[:, None, :], rconv0, conv_w, conv_b[None, :],
                         pw["wa_bd"], ba[None, :], pw["wx_bd"], bx[None, :], lam[None, :], tr)
    assert t >= RNN_CONV - 1 and t >= FFN_CONV - 1
    rconv_new = b3(xr)[:, t - (RNN_CONV - 1):, :]

    tf = min(512, t)
    y, fconv_new = _ffn(x, attn, rnn, fconv0, pw["wo_a"], pw["wo_r"], norm_ffn[None, :], pw["w_up"],
                        pw["w_gate"], fconv_w, fconv_b[None, :], pw["w_down"], norm_last[None, :],
                        tf, final_norm)
    k_new = k.reshape(b, t, N_KV_HEADS, HEAD_DIM)
    v_new = v.reshape(b, t, N_KV_HEADS, HEAD_DIM)
    return y, (k_new, v_new, b3(ki), h_last[:, 0, :], rconv_new, fconv_new)


def kernel(x_prompt, x_sample, cache_k, cache_v, cache_kidx, state_rglru_h, state_rglru_conv,
           state_ffn_conv, norm_mix, w_in, rglru_conv_w, rglru_conv_b, rglru_wa, rglru_ba,
           rglru_wx, rglru_bx, rglru_lambda, rel_bias, w_out, norm_ffn, w_ffn_up, w_ffn_gate,
           ffn_conv_w, ffn_conv_b, w_ffn_down, norm_final):
    depth = w_in.shape[0]
    bp = x_prompt.shape[0]
    dt = x_prompt.dtype
    xp, xs = x_prompt, x_sample
    outs_p, outs_s = [], []
    for i in range(depth):
        pw = _pack_layer_weights(w_in[i], rglru_wa[i], rglru_wx[i], w_out[i], w_ffn_up[i],
                                 w_ffn_gate[i], w_ffn_down[i])
        r = pw["rnn_width"]
        last = i == depth - 1
        lw = (pw, norm_mix[i], rglru_conv_w[i], rglru_conv_b[i], rglru_ba[i], rglru_bx[i],
              rglru_lambda[i], rel_bias, norm_ffn[i], ffn_conv_w[i], ffn_conv_b[i], norm_final, last)
        xp, st_p = _layer(xp,
                          jnp.zeros((bp, 0, N_KV_HEADS, HEAD_DIM), dt),
                          jnp.zeros((bp, 0, N_KV_HEADS, HEAD_DIM), dt),
                          jnp.zeros((bp, 0, IDX_DIM), dt),
                          jnp.zeros((bp, r), dt),
                          jnp.zeros((bp, RNN_CONV - 1, r), dt),
                          jnp.zeros((bp, FFN_CONV - 1, w_ffn_up.shape[-1]), dt),
                          *lw)
        xs, st_s = _layer(xs, cache_k[i], cache_v[i], cache_kidx[i], state_rglru_h[i],
                          state_rglru_conv[i], state_ffn_conv[i], *lw)
        outs_p.append(st_p)
        outs_s.append(st_s)
    stack = lambda outs, j: jnp.stack([o[j] for o in outs])
    return ((xp, xs) + tuple(stack(outs_p, j) for j in range(6))
            + tuple(stack(outs_s, j) for j in range(6)))
```

```python
import functools
import math

import jax
import jax.numpy as jnp
import numpy as np
from jax import lax
from jax.experimental import pallas as pl
from jax.experimental.pallas import tpu as pltpu

F32 = jnp.float32
BF16 = jnp.bfloat16
I32 = jnp.int32

CHUNK = 64
N_HEADS = 8
HEAD_DIM = 64
N_KV_HEADS = 2
GQA_GROUP = N_HEADS // N_KV_HEADS
ATTN_WIDTH = N_HEADS * HEAD_DIM
N_IDX_HEADS = 8
IDX_DIM = 64
TOPK_MAX = 256
NUM_BUCKETS = 32
MAX_DISTANCE = 128
RNN_BLOCKS = 8
RNN_CONV = 4
RGLRU_C = 8.0
FFN_CONV = 3
EPS = 1e-6

LANES = 128
SUBLANES = 8
VMEM_LIMIT_BYTES = 56 * 1024 * 1024

Q_TILE = 128
KEY_TILE = 256
SEL_CHUNK = 2 * KEY_TILE
SEL_ACCUMULATORS = 8
MASKED = -1e30
INT_MIN = -(2 ** 31)
INT_MAX = 2 ** 31 - 1
F32_MIN_NORMAL_KEY = 2 ** 23
LOG2_E = 1.4426950408889634

_C_Q = 0
_C_K = _C_Q + ATTN_WIDTH
_C_V = _C_K + LANES
_C_QI = _C_V + LANES
_C_KW = _C_QI + N_IDX_HEADS * IDX_DIM
_C_KK = _C_KW + LANES
_C_XR = _C_KK + LANES


def _rmsnorm(x, g):
    y = x * lax.rsqrt(jnp.mean(x * x, axis=-1, keepdims=True) + EPS)
    return y * g


def _proj_kernel(x_ref, g_ref, w_ref, q_ref, k_ref, v_ref, kb_ref, vb_ref, qi_ref, ki_ref,
                 kk_ref, kw_ref, xr_ref, gr_ref, *, rnn_width):
    h = _rmsnorm(x_ref[...], g_ref[...]).astype(BF16)

    def proj(start, width):
        return jnp.dot(h, w_ref[:, start:start + width], preferred_element_type=F32)

    q_ref[...] = (proj(_C_Q, ATTN_WIDTH) * (HEAD_DIM ** -0.5)).astype(BF16)
    k = proj(_C_K, LANES)
    v = proj(_C_V, LANES)
    k_ref[...] = k
    v_ref[...] = v
    kb_ref[...] = k.astype(BF16)
    vb_ref[...] = v.astype(BF16)
    qi_ref[...] = proj(_C_QI, N_IDX_HEADS * IDX_DIM).astype(BF16)
    kw = proj(_C_KW, LANES)
    kw_ref[...] = kw
    ki_ref[...] = kw[:, :IDX_DIM]
    kk_ref[...] = proj(_C_KK, LANES).astype(BF16)
    xr_ref[...] = proj(_C_XR, rnn_width)
    gr_ref[...] = proj(_C_XR + rnn_width, rnn_width)


def _project(x2d, g, w_packed, rnn_width, tm):
    n, d = x2d.shape
    ncol = w_packed.shape[1]
    row = lambda w: pl.BlockSpec((tm, w), lambda i: (i, 0))
    out_shape = (
        jax.ShapeDtypeStruct((n, ATTN_WIDTH), BF16),
        jax.ShapeDtypeStruct((n, LANES), F32),
        jax.ShapeDtypeStruct((n, LANES), F32),
        jax.ShapeDtypeStruct((n, LANES), BF16),
        jax.ShapeDtypeStruct((n, LANES), BF16),
        jax.ShapeDtypeStruct((n, N_IDX_HEADS * IDX_DIM), BF16),
        jax.ShapeDtypeStruct((n, IDX_DIM), F32),
        jax.ShapeDtypeStruct((n, LANES), BF16),
        jax.ShapeDtypeStruct((n, LANES), F32),
        jax.ShapeDtypeStruct((n, rnn_width), F32),
        jax.ShapeDtypeStruct((n, rnn_width), F32),
    )
    out_specs = (row(ATTN_WIDTH), row(LANES), row(LANES), row(LANES), row(LANES),
                 row(N_IDX_HEADS * IDX_DIM), row(IDX_DIM), row(LANES), row(LANES),
                 row(rnn_width), row(rnn_width))
    return pl.pallas_call(
        functools.partial(_proj_kernel, rnn_width=rnn_width),
        grid=(n // tm,),
        in_specs=[row(d), pl.BlockSpec((1, d), lambda i: (0, 0)),
                  pl.BlockSpec((d, ncol), lambda i: (0, 0))],
        out_specs=out_specs,
        out_shape=out_shape,
        compiler_params=pltpu.CompilerParams(dimension_semantics=("arbitrary",),
                                             vmem_limit_bytes=VMEM_LIMIT_BYTES),
        name="proj",
    )(x2d, g, w_packed)


def _bucket_tables():
    half = NUM_BUCKETS // 2
    max_exact = half // 2
    r = np.arange(Q_TILE, dtype=np.int64)[None, :]
    c = np.arange(KEY_TILE, dtype=np.int64)[:, None]
    tabs = []
    for i in range(3):
        rel = (i - 2) * LANES + c - r
        side = np.where(rel > 0, half, 0)
        n = np.abs(rel)
        nf = np.maximum(n, 1).astype(np.float32)
        ratio = np.log(nf / np.float32(max_exact)) / np.float32(math.log(MAX_DISTANCE / max_exact))
        large = max_exact + (ratio * np.float32(half - max_exact)).astype(np.int32)
        large = np.minimum(large, half - 1)
        tabs.append(side + np.where(n < max_exact, n, large))
    return np.stack(tabs).astype(np.int32)


def _size_classes(total_rows):
    return tuple(range(SEL_CHUNK, total_rows + 1, SEL_CHUNK))


def _skip_subnormal_keys(k, min_normal):
    k = jnp.where((k > 0) & (k < min_normal), min_normal, k)
    return jnp.where((k < 0) & (k > -min_normal), 0, k)


def _key_to_float(k):
    k = _skip_subnormal_keys(k, F32_MIN_NORMAL_KEY)
    bits = jnp.where(k < 0, (-k) | INT_MIN, k)
    return lax.bitcast_convert_type(bits, F32)


def _attn_kernel(rb_ref, bkt_ref, q_ref, qi_ref, kw_ref, kk_ref, kb_ref, vt_ref, o_ref,
                 sc_ref, qit_ref, qt_ref, tbl_ref, s_ref, mx_ref, den_ref, acc_ref,
                 thr_ref, ngt_ref, neq_ref, tie_ref,
                 *, past, n_keys, top_k):
    tq, lt, ct = Q_TILE, KEY_TILE, SEL_CHUNK
    far_bucket = NUM_BUCKETS // 2 - 1
    j = pl.program_id(1)
    q0 = past + j * tq

    @pl.when((pl.program_id(0) == 0) & (j == 0))
    def _():
        for i in range(3):
            bkt = bkt_ref[i]
            for slot, h in enumerate(_HEAD_ORDER):
                t = jnp.zeros((lt, tq), F32)
                for b in range(NUM_BUCKETS):
                    t = jnp.where(bkt == b, rb_ref[b, h], t)
                tbl_ref[i, slot] = t - rb_ref[far_bucket, h]

    lo_rows = lax.broadcasted_iota(I32, (LANES, tq), 0) < HEAD_DIM
    for c in range(N_HEADS // 2):
        qi_t = qi_ref[:, c * LANES:(c + 1) * LANES].astype(F32).T
        q_t = q_ref[:, c * LANES:(c + 1) * LANES].astype(F32).T
        for half in range(2):
            h = 2 * c + half
            keep = lo_rows if half == 0 else jnp.logical_not(lo_rows)
            qit_ref[:, h * tq:(h + 1) * tq] = jnp.where(keep, qi_t, 0.0).astype(BF16)
            qt_ref[:, h * tq:(h + 1) * tq] = jnp.where(keep, q_t, 0.0).astype(BF16)

    wi_t = kw_ref[...].T[IDX_DIM:IDX_DIM + N_IDX_HEADS, :]
    wi_t = (wi_t * (N_IDX_HEADS ** -0.5)) * (IDX_DIM ** -0.5)

    qpos = q0 + lax.broadcasted_iota(I32, (1, tq), 1)
    n_adm = jnp.minimum(((qpos // CHUNK) + 1) * CHUNK, n_keys)
    n_adm_max = jnp.minimum(((q0 + tq - 1) // CHUNK + 1) * CHUNK, n_keys)
    n_kt = (n_adm_max + lt - 1) // lt
    n_ct = (n_adm_max + ct - 1) // ct
    crow = lax.broadcasted_iota(I32, (ct, tq), 0)

    def for_tiles(first, stop, span_fn):
        n = stop - first

        def quads(i, carry):
            span_fn(pl.multiple_of((first + 4 * i) * lt, lt), 4)
            return carry

        lax.fori_loop(0, n // 4, quads, 0)

        @pl.when((n & 2) != 0)
        def _():
            span_fn(pl.multiple_of((first + (n // 4) * 4) * lt, lt), 2)

        @pl.when((n & 1) != 0)
        def _():
            span_fn(pl.multiple_of((stop - 1) * lt, lt), 1)

    def score_span(base, tiles):
        rows = tiles * lt
        s = jnp.dot(kk_ref[pl.ds(base, rows), :], qit_ref[...], preferred_element_type=F32)
        acc = jnp.zeros((rows, tq), F32)
        for h in range(N_IDX_HEADS):
            acc = acc + wi_t[h:h + 1, :] * jnp.maximum(s[:, h * tq:(h + 1) * tq], 0.0)
        key_pos = lax.broadcasted_iota(I32, (rows, tq), 0) + base
        sc_ref[pl.ds(base, rows), :] = jnp.where(key_pos < n_adm, acc, -jnp.inf)

    for_tiles(0, n_kt, score_span)

    def select(rows):
        def blank(kt, carry):
            base = pl.multiple_of(kt * lt, lt)
            sc_ref[pl.ds(base, lt), :] = jnp.full((lt, tq), -jnp.inf, F32)
            return carry

        lax.fori_loop(n_kt, rows // lt, blank, 0)

        def count(cand):
            parts = [None] * SEL_ACCUMULATORS
            for r in range(rows // SUBLANES):
                hit = jnp.where(sc_ref[r * SUBLANES:(r + 1) * SUBLANES, :] >= cand, 1, 0)
                a = parts[r % SEL_ACCUMULATORS]
                parts[r % SEL_ACCUMULATORS] = hit if a is None else a + hit
            total = functools.reduce(lambda a, b: a + b, [a for a in parts if a is not None])
            return jnp.sum(total, axis=0, keepdims=True)

        def step(i, t):
            cand = t + (jnp.int32(1) << (31 - i))
            return jnp.where(count(_key_to_float(cand)) >= top_k, cand, t)

        t = lax.fori_loop(0, 32, step, jnp.full((1, tq), INT_MIN, I32))
        thr = _key_to_float(t)
        n_gt = count(_key_to_float(t + 1))
        thr_ref[...] = thr
        ngt_ref[...] = n_gt
        neq_ref[...] = count(thr) - n_gt

    classes = _size_classes(sc_ref.shape[0])
    for below, rows in zip((0,) + classes[:-1], classes):
        pl.when((n_adm_max > below) & (n_adm_max <= rows))(functools.partial(select, rows))

    thr = thr_ref[...]
    n_gt = ngt_ref[...]
    n_eq = neq_ref[...]

    def count32(pred):
        def body(c, cnt):
            base = pl.multiple_of(c * ct, ct)
            hit = pred(sc_ref[pl.ds(base, ct), :], base).astype(I32)
            return cnt + jnp.sum(hit.reshape(ct // SUBLANES, SUBLANES, tq), axis=0)
        cnt = lax.fori_loop(0, n_ct, body, jnp.zeros((SUBLANES, tq), I32))
        return jnp.sum(cnt, axis=0, keepdims=True)

    want = top_k - n_gt
    finite_thr = thr > -jnp.inf
    need = (n_eq > want) & finite_thr & (qpos < n_keys)
    tie_ref[...] = jnp.where(finite_thr, INT_MAX, -1)

    @pl.when(jnp.max(need.astype(I32)) > 0)
    def _():
        nbits = max(1, int(sc_ref.shape[0]).bit_length())

        def tie_step(i, m):
            cand = m | (jnp.int32(1) << (nbits - 1 - i))
            c = count32(lambda blk, base: (blk == thr) & (crow + base < cand))
            return jnp.where(c < want, cand, m)

        m = lax.fori_loop(0, nbits, tie_step, jnp.zeros((1, tq), I32))
        tie_ref[...] = jnp.where(need, m, tie_ref[...])

    tie_max = tie_ref[...]

    n_far = jnp.clip((q0 - (LANES - 1)) // lt, 0, n_kt)

    def logits_span(base, tiles, near=False):
        rows = tiles * lt
        blk = sc_ref[pl.ds(base, rows), :]
        key_pos = lax.broadcasted_iota(I32, (rows, tq), 0) + base
        sel = (blk > thr) | ((blk == thr) & (key_pos <= tie_max))
        mask = jnp.where(sel, 0.0, MASKED)
        s = jnp.dot(kb_ref[pl.ds(base, rows), :], qt_ref[...], preferred_element_type=F32)
        first_table = (base - q0) // LANES + 2
        tops = []
        for h in range(N_HEADS):
            sh = s[:, h * tq:(h + 1) * tq] + mask
            if near:
                sh = sh + jnp.concatenate([tbl_ref[first_table + 2 * t, h] for t in range(tiles)], axis=0)
            sh = sh * LOG2_E
            s_ref[pl.ds(base, rows), h * tq:(h + 1) * tq] = sh
            tops.append(jnp.max(sh.reshape(rows // SUBLANES, SUBLANES, tq), axis=0))
        mx_ref[...] = jnp.maximum(mx_ref[...], jnp.concatenate(tops, axis=1))

    mx_ref[...] = jnp.full(mx_ref.shape, MASKED * LOG2_E, F32)
    for_tiles(0, n_far, logits_span)
    for tiles in (1, 2):
        pl.when(n_kt - n_far == tiles)(functools.partial(
            logits_span, pl.multiple_of(n_far * lt, lt), tiles, near=True))
    m_fin = jnp.max(mx_ref[...], axis=0, keepdims=True)

    acc_ref[...] = jnp.zeros(acc_ref.shape, F32)
    den_ref[...] = jnp.zeros(den_ref.shape, F32)

    def pv_span(base, tiles):
        rows = tiles * lt
        p = jnp.exp2(s_ref[pl.ds(base, rows), :] - m_fin)
        kt = base // lt
        values_t = jnp.concatenate([vt_ref[kt + t] for t in range(tiles)], axis=1)
        acc_ref[...] += jnp.dot(values_t, p.astype(BF16), preferred_element_type=F32)
        den_ref[...] += jnp.sum(p.reshape(rows // SUBLANES, SUBLANES, N_HEADS * tq), axis=0)

    for_tiles(0, n_kt, pv_span)
    out_t = acc_ref[...] / jnp.sum(den_ref[...], axis=0, keepdims=True)
    lo_lanes = lax.broadcasted_iota(I32, (tq, LANES), 1) < HEAD_DIM
    for c in range(N_HEADS // 2):
        lo = out_t[:, (2 * c) * tq:(2 * c + 1) * tq].T
        hi = out_t[:, (2 * c + 1) * tq:(2 * c + 2) * tq].T
        o_ref[:, c * LANES:(c + 1) * LANES] = jnp.where(lo_lanes, lo, hi).astype(o_ref.dtype)


def _attention(rel_bias, q, qi, kw, kk, kb, vb, *, past, n_keys):
    b, tq_total, _ = q.shape
    lp = kk.shape[1]
    top_k = min(TOPK_MAX, n_keys // 4)
    assert tq_total % Q_TILE == 0 and lp % KEY_TILE == 0 and past % KEY_TILE == 0
    assert lp >= n_keys and top_k <= KEY_TILE
    n_kt = lp // KEY_TILE
    lc = _round_up(lp, SEL_CHUNK)
    vt = vb.reshape(b, n_kt, KEY_TILE, LANES).swapaxes(2, 3)
    bkt = jnp.asarray(_bucket_tables())
    qspec = lambda w: pl.BlockSpec((None, Q_TILE, w), lambda bi, j: (bi, j, 0))
    kspec = pl.BlockSpec((None, lp, LANES), lambda bi, j: (bi, 0, 0))
    return pl.pallas_call(
        functools.partial(_attn_kernel, past=past, n_keys=n_keys, top_k=top_k),
        grid=(b, tq_total // Q_TILE),
        in_specs=[pl.BlockSpec(memory_space=pltpu.SMEM),
                  pl.BlockSpec((3, KEY_TILE, Q_TILE), lambda bi, j: (0, 0, 0)),
                  qspec(ATTN_WIDTH), qspec(N_IDX_HEADS * IDX_DIM), qspec(LANES),
                  kspec, kspec,
                  pl.BlockSpec((None, n_kt, LANES, KEY_TILE), lambda bi, j: (bi, 0, 0, 0))],
        out_specs=qspec(ATTN_WIDTH),
        out_shape=jax.ShapeDtypeStruct((b, tq_total, ATTN_WIDTH), BF16),
        scratch_shapes=[
            pltpu.VMEM((lc, Q_TILE), F32),
            pltpu.VMEM((LANES, N_IDX_HEADS * Q_TILE), BF16),
            pltpu.VMEM((LANES, N_HEADS * Q_TILE), BF16),
            pltpu.VMEM((3, N_HEADS, KEY_TILE, Q_TILE), F32),
            pltpu.VMEM((lp, N_HEADS * Q_TILE), F32),
            pltpu.VMEM((SUBLANES, N_HEADS * Q_TILE), F32),
            pltpu.VMEM((SUBLANES, N_HEADS * Q_TILE), F32),
            pltpu.VMEM((LANES, N_HEADS * Q_TILE), F32),
            pltpu.VMEM((1, Q_TILE), F32),
            pltpu.VMEM((1, Q_TILE), I32),
            pltpu.VMEM((1, Q_TILE), I32),
            pltpu.VMEM((1, Q_TILE), I32),
        ],
        compiler_params=pltpu.CompilerParams(dimension_semantics=("arbitrary", "arbitrary"),
                                             vmem_limit_bytes=VMEM_LIMIT_BYTES),
        name="attn",
    )(rel_bias, bkt, q, qi, kw, kk, kb, vt)


def _softplus(x):
    return jnp.maximum(x, 0.0) + jnp.log1p(jnp.exp(-jnp.abs(x)))


def _rglru_kernel(xr_ref, gr_ref, h0_ref, c0_ref, cw_ref, cb_ref, wa_ref, ba_ref, wx_ref, bx_ref,
                  lam_ref, rnn_ref, hl_ref, xs_ref, a_ref, u_ref, hs_ref, hc_ref, *, tm):
    t = pl.program_id(1)
    pad = SUBLANES
    hist = RNN_CONV - 1

    @pl.when(t == 0)
    def _():
        xs_ref[0:pad, :] = jnp.zeros((pad, xs_ref.shape[-1]), F32)
        xs_ref[pad - hist:pad, :] = c0_ref[...]
        hc_ref[...] = jnp.broadcast_to(h0_ref[...], hc_ref.shape)

    xs_ref[pad:pad + tm, :] = xr_ref[...]
    groups = (tm // SUBLANES, SUBLANES, xs_ref.shape[-1])
    xe = xs_ref[...].reshape((groups[0] + 1,) + groups[1:])
    row = lax.broadcasted_iota(I32, groups, 1)
    xc = cb_ref[...]
    for jj in range(RNN_CONV):
        k = hist - jj
        if k:
            rolled = pltpu.roll(xe, k, 1)
            delayed = jnp.where(row < k, rolled[:-1], rolled[1:])
        else:
            delayed = xe[1:]
        xc = xc + cw_ref[jj:jj + 1, :] * delayed
    xc = xc.reshape(tm, -1)
    xs_ref[0:pad, :] = xs_ref[tm:tm + pad, :]

    xcb = xc.astype(BF16)
    rg = jax.nn.sigmoid(jnp.dot(xcb, wa_ref[...], preferred_element_type=F32) + ba_ref[...])
    ig = jax.nn.sigmoid(jnp.dot(xcb, wx_ref[...], preferred_element_type=F32) + bx_ref[...])
    log_a = (-RGLRU_C * rg) * _softplus(-lam_ref[...])
    a = jnp.exp(log_a)
    u = jnp.sqrt(-jnp.tanh(log_a) * (a * a + 1.0)) * (ig * xc)

    groups = (tm // SUBLANES, SUBLANES, a.shape[-1])
    a = a.reshape(groups)
    u = u.reshape(groups)
    row = lax.broadcasted_iota(I32, groups, 1)
    for s in (1, 2, 4):
        a_prev = pltpu.roll(a, s, 1)
        u_prev = pltpu.roll(u, s, 1)
        use = row >= s
        u = jnp.where(use, a * u_prev + u, u)
        a = jnp.where(use, a * a_prev, a)
    a_ref[...] = a.reshape(tm, -1)
    u_ref[...] = u.reshape(tm, -1)

    def carry_body(g, h):
        base = pl.multiple_of(g * SUBLANES, SUBLANES)
        hg = a_ref[pl.ds(base, SUBLANES), :] * h + u_ref[pl.ds(base, SUBLANES), :]
        hs_ref[pl.ds(base, SUBLANES), :] = hg
        return jnp.broadcast_to(hg[SUBLANES - 1:SUBLANES, :], hg.shape)

    h_end = lax.fori_loop(0, tm // SUBLANES, carry_body, hc_ref[...])
    hc_ref[...] = h_end
    rnn_ref[...] = (hs_ref[...] * jax.nn.gelu(gr_ref[...])).astype(rnn_ref.dtype)

    @pl.when(t == pl.num_programs(1) - 1)
    def _():
        hl_ref[...] = h_end[0:1, :]


def _rglru(xr, gr, h0, conv0, conv_w, conv_b, wa_bd, ba, wx_bd, bx, lam, tm):
    b, t, r = xr.shape
    assert t % tm == 0 and tm % SUBLANES == 0
    tile = pl.BlockSpec((None, tm, r), lambda bi, ti: (bi, ti, 0))
    per_b = lambda rows: pl.BlockSpec((None, rows, r), lambda bi, ti: (bi, 0, 0))
    full = lambda a: pl.BlockSpec(a.shape, lambda bi, ti: (0,) * a.ndim)
    return pl.pallas_call(
        functools.partial(_rglru_kernel, tm=tm),
        grid=(b, t // tm),
        in_specs=[tile, tile, per_b(1), per_b(RNN_CONV - 1), full(conv_w), full(conv_b),
                  full(wa_bd), full(ba), full(wx_bd), full(bx), full(lam)],
        out_specs=(tile, per_b(1)),
        out_shape=(jax.ShapeDtypeStruct((b, t, r), BF16), jax.ShapeDtypeStruct((b, 1, r), F32)),
        scratch_shapes=[pltpu.VMEM((tm + SUBLANES, r), F32), pltpu.VMEM((tm, r), F32),
                        pltpu.VMEM((tm, r), F32), pltpu.VMEM((tm, r), F32),
                        pltpu.VMEM((SUBLANES, r), F32)],
        compiler_params=pltpu.CompilerParams(dimension_semantics=("arbitrary", "arbitrary"),
                                             vmem_limit_bytes=VMEM_LIMIT_BYTES),
        name="rglru",
    )(xr, gr, h0, conv0, conv_w, conv_b, wa_bd, ba, wx_bd, bx, lam)


def _ffn_kernel(x_ref, at_ref, rn_ref, c0_ref, woa_ref, wor_ref, gf_ref, wu_ref, wg_ref, cw_ref,
                cb_ref, wd_ref, gl_ref, y_ref, fc_ref, up_ref, *, tm, final_norm):
    t = pl.program_id(1)
    pad = SUBLANES
    hist = FFN_CONV - 1

    @pl.when(t == 0)
    def _():
        up_ref[pad - hist:pad, :] = c0_ref[...]

    x1 = (x_ref[...]
          + jnp.dot(at_ref[...], woa_ref[...], preferred_element_type=F32)
          + jnp.dot(rn_ref[...], wor_ref[...], preferred_element_type=F32))
    f = _rmsnorm(x1, gf_ref[...]).astype(BF16)
    up_ref[pad:pad + tm, :] = jnp.dot(f, wu_ref[...], preferred_element_type=F32)
    up = cb_ref[...] + cw_ref[0:1, :] * up_ref[pad - hist:pad - hist + tm, :]
    for jj in range(1, FFN_CONV):
        up = up + cw_ref[jj:jj + 1, :] * up_ref[pad - hist + jj:pad - hist + jj + tm, :]
    tail = up_ref[pad + tm - hist:pad + tm, :]
    up_ref[pad - hist:pad, :] = tail
    gate = jnp.dot(f, wg_ref[...], preferred_element_type=F32)
    act = (jax.nn.gelu(up) * gate).astype(BF16)
    x2 = x1 + jnp.dot(act, wd_ref[...], preferred_element_type=F32)
    y_ref[...] = _rmsnorm(x2, gl_ref[...]) if final_norm else x2

    @pl.when(t == pl.num_programs(1) - 1)
    def _():
        fc_ref[...] = tail


def _ffn(x, attn, rnn, conv0, wo_a, wo_r, g_ffn, w_up, w_gate, conv_w, conv_b, w_down, g_last,
         tm, final_norm):
    b, t, d = x.shape
    dff = w_up.shape[1]
    assert t % tm == 0
    tile = lambda w: pl.BlockSpec((None, tm, w), lambda bi, ti: (bi, ti, 0))
    per_b = pl.BlockSpec((None, FFN_CONV - 1, dff), lambda bi, ti: (bi, 0, 0))
    full = lambda a: pl.BlockSpec(a.shape, lambda bi, ti: (0,) * a.ndim,
                                  pipeline_mode=pl.Buffered(1))
    return pl.pallas_call(
        functools.partial(_ffn_kernel, tm=tm, final_norm=final_norm),
        grid=(b, t // tm),
        in_specs=[tile(d), tile(attn.shape[-1]), tile(rnn.shape[-1]), per_b, full(wo_a), full(wo_r),
                  full(g_ffn), full(w_up), full(w_gate), full(conv_w), full(conv_b), full(w_down),
                  full(g_last)],
        out_specs=(tile(d), per_b),
        out_shape=(jax.ShapeDtypeStruct((b, t, d), F32),
                   jax.ShapeDtypeStruct((b, FFN_CONV - 1, dff), F32)),
        scratch_shapes=[pltpu.VMEM((tm + SUBLANES, dff), F32)],
        compiler_params=pltpu.CompilerParams(dimension_semantics=("arbitrary", "arbitrary"),
                                             vmem_limit_bytes=VMEM_LIMIT_BYTES),
        name="ffn",
    )(x, attn, rnn, conv0, wo_a, wo_r, g_ffn, w_up, w_gate, conv_w, conv_b, w_down, g_last)


_HEAD_ORDER = tuple(h for c in range(GQA_GROUP) for h in (c, GQA_GROUP + c))


def _pack_layer_weights(w_in, wa, wx, w_out, w_up, w_gate, w_down):
    d = w_in.shape[0]
    sizes = [ATTN_WIDTH, N_KV_HEADS * HEAD_DIM, N_KV_HEADS * HEAD_DIM, N_IDX_HEADS * IDX_DIM,
             IDX_DIM, N_IDX_HEADS]
    offs = np.cumsum([0] + sizes)
    wq, wk, wv, wqi, wki, wwi = (w_in[:, offs[i]:offs[i + 1]] for i in range(6))
    wrest = w_in[:, offs[6]:]
    rnn_width = wrest.shape[1] // 2
    order = np.asarray(_HEAD_ORDER)
    wq = wq.reshape(d, N_HEADS, HEAD_DIM)[:, order, :].reshape(d, ATTN_WIDTH)
    zpad = jnp.zeros((d, LANES - IDX_DIM - N_IDX_HEADS), w_in.dtype)
    packed = jnp.concatenate([wq, wk, wv, wqi, wki, wwi, zpad, wki, wki, wrest], axis=1).astype(BF16)
    eye = jnp.eye(RNN_BLOCKS, dtype=wa.dtype)
    bd = lambda w: jnp.einsum("nij,nm->nimj", w, eye).reshape(rnn_width, rnn_width).astype(BF16)
    wo_a = w_out[:ATTN_WIDTH].reshape(N_HEADS, HEAD_DIM, -1)[order].reshape(ATTN_WIDTH, -1)
    wo_r = w_out[ATTN_WIDTH:]
    return dict(w_packed=packed, rnn_width=rnn_width, wa_bd=bd(wa), wx_bd=bd(wx),
                wo_a=wo_a.astype(BF16), wo_r=wo_r.astype(BF16), w_up=w_up.astype(BF16),
                w_gate=w_gate.astype(BF16), w_down=w_down.astype(BF16))


def _pad_rows(a, rows):
    return a if a.shape[1] == rows else jnp.pad(a, ((0, 0), (0, rows - a.shape[1]), (0, 0)))


def _round_up(n, m):
    return -(-n // m) * m


def _layer(x, ck, cv, ckidx, h0, rconv0, fconv0, pw, norm_mix, conv_w, conv_b, ba, bx, lam,
           rel_bias, norm_ffn, fconv_w, fconv_b, norm_last, final_norm):
    b, t, d = x.shape
    past = ck.shape[1]
    r = pw["rnn_width"]
    n_tok = b * t
    tm = min(512, n_tok)
    (q, k, v, kb, vb, qi, ki, kk, kw, xr, gr) = _project(
        x.reshape(n_tok, d), norm_mix[None, :], pw["w_packed"], r, tm)
    b3 = lambda a: a.reshape(b, t, a.shape[-1])

    n_keys = past + t
    lp = _round_up(n_keys, KEY_TILE)
    tq = _round_up(t, Q_TILE)
    if past:
        ck2 = ck.reshape(b, past, -1).astype(BF16)
        cv2 = cv.reshape(b, past, -1).astype(BF16)
        cki = ckidx.astype(BF16)
        keys_k = jnp.concatenate([ck2, b3(kb)], axis=1)
        keys_v = jnp.concatenate([cv2, b3(vb)], axis=1)
        keys_i = jnp.concatenate([jnp.concatenate([cki, cki], axis=-1), b3(kk)], axis=1)
    else:
        keys_k, keys_v, keys_i = b3(kb), b3(vb), b3(kk)
    attn = _attention(rel_bias, _pad_rows(b3(q), tq), _pad_rows(b3(qi), tq), _pad_rows(b3(kw), tq),
                      _pad_rows(keys_i, lp), _pad_rows(keys_k, lp), _pad_rows(keys_v, lp),
                      past=past, n_keys=n_keys)[:, :t]

    tr = min(512, t)
    rnn, h_last = _rglru(b3(xr), b3(gr), h0[:, None, :], rconv0, conv_w, conv_b[None, :],
                         pw["wa_bd"], ba[None, :], pw["wx_bd"], bx[None, :], lam[None, :], tr)
    assert t >= RNN_CONV - 1 and t >= FFN_CONV - 1
    rconv_new = b3(xr)[:, t - (RNN_CONV - 1):, :]

    tf = min(512, t)
    y, fconv_new = _ffn(x, attn, rnn, fconv0, pw["wo_a"], pw["wo_r"], norm_ffn[None, :], pw["w_up"],
                        pw["w_gate"], fconv_w, fconv_b[None, :], pw["w_down"], norm_last[None, :],
                        tf, final_norm)
    k_new = k.reshape(b, t, N_KV_HEADS, HEAD_DIM)
    v_new = v.reshape(b, t, N_KV_HEADS, HEAD_DIM)
    return y, (k_new, v_new, b3(ki), h_last[:, 0, :], rconv_new, fconv_new)


def kernel(x_prompt, x_sample, cache_k, cache_v, cache_kidx, state_rglru_h, state_rglru_conv,
           state_ffn_conv, norm_mix, w_in, rglru_conv_w, rglru_conv_b, rglru_wa, rglru_ba,
           rglru_wx, rglru_bx, rglru_lambda, rel_bias, w_out, norm_ffn, w_ffn_up, w_ffn_gate,
           ffn_conv_w, ffn_conv_b, w_ffn_down, norm_final):
    depth = w_in.shape[0]
    bp = x_prompt.shape[0]
    dt = x_prompt.dtype
    xp, xs = x_prompt, x_sample
    outs_p, outs_s = [], []
    for i in range(depth):
        pw = _pack_layer_weights(w_in[i], rglru_wa[i], rglru_wx[i], w_out[i], w_ffn_up[i],
                                 w_ffn_gate[i], w_ffn_down[i])
        r = pw["rnn_width"]
        last = i == depth - 1
        lw = (pw, norm_mix[i], rglru_conv_w[i], rglru_conv_b[i], rglru_ba[i], rglru_bx[i],
              rglru_lambda[i], rel_bias, norm_ffn[i], ffn_conv_w[i], ffn_conv_b[i], norm_final, last)
        xp, st_p = _layer(xp,
                          jnp.zeros((bp, 0, N_KV_HEADS, HEAD_DIM), dt),
                          jnp.zeros((bp, 0, N_KV_HEADS, HEAD_DIM), dt),
                          jnp.zeros((bp, 0, IDX_DIM), dt),
                          jnp.zeros((bp, r), dt),
                          jnp.zeros((bp, RNN_CONV - 1, r), dt),
                          jnp.zeros((bp, FFN_CONV - 1, w_ffn_up.shape[-1]), dt),
                          *lw)
        xs, st_s = _layer(xs, cache_k[i], cache_v[i], cache_kidx[i], state_rglru_h[i],
                          state_rglru_conv[i], state_ffn_conv[i], *lw)
        outs_p.append(st_p)
        outs_s.append(st_s)
    stack = lambda outs, j: jnp.stack([o[j] for o in outs])
    return ((xp, xs) + tuple(stack(outs_p, j) for j in range(6))
            + tuple(stack(outs_s, j) for j in range(6)))
```

```python
import functools
import math

import jax
import jax.numpy as jnp
import numpy as np
from jax import lax
from jax.experimental import pallas as pl
from jax.experimental.pallas import tpu as pltpu

F32 = jnp.float32
BF16 = jnp.bfloat16
I32 = jnp.int32

CHUNK = 64
N_HEADS = 8
HEAD_DIM = 64
N_KV_HEADS = 2
GQA_GROUP = N_HEADS // N_KV_HEADS
ATTN_WIDTH = N_HEADS * HEAD_DIM
N_IDX_HEADS = 8
IDX_DIM = 64
TOPK_MAX = 256
NUM_BUCKETS = 32
MAX_DISTANCE = 128
RNN_BLOCKS = 8
RNN_CONV = 4
RGLRU_C = 8.0
FFN_CONV = 3
EPS = 1e-6

LANES = 128
SUBLANES = 8
VMEM_LIMIT_BYTES = 56 * 1024 * 1024

Q_TILE = 128
KEY_TILE = 256
MAX_SPAN = 8
SEL_CHUNK = 2 * KEY_TILE
SEL_ACCUMULATORS = 8
MASKED = -1e30
INT_MIN = -(2 ** 31)
INT_MAX = 2 ** 31 - 1
F32_MIN_NORMAL_KEY = 2 ** 23
LOG2_E = 1.4426950408889634

_C_Q = 0
_C_K = _C_Q + ATTN_WIDTH
_C_V = _C_K + LANES
_C_QI = _C_V + LANES
_C_KW = _C_QI + N_IDX_HEADS * IDX_DIM
_C_KK = _C_KW + LANES
_C_XR = _C_KK + LANES


def _rmsnorm(x, g):
    y = x * lax.rsqrt(jnp.mean(x * x, axis=-1, keepdims=True) + EPS)
    return y * g


def _proj_kernel(x_ref, g_ref, w_ref, q_ref, k_ref, v_ref, kb_ref, vb_ref, qi_ref, ki_ref,
                 kk_ref, kw_ref, xr_ref, gr_ref, *, rnn_width):
    h = _rmsnorm(x_ref[...], g_ref[...]).astype(BF16)

    def proj(start, width):
        return jnp.dot(h, w_ref[:, start:start + width], preferred_element_type=F32)

    q_ref[...] = (proj(_C_Q, ATTN_WIDTH) * (HEAD_DIM ** -0.5)).astype(BF16)
    k = proj(_C_K, LANES)
    v = proj(_C_V, LANES)
    k_ref[...] = k
    v_ref[...] = v
    kb_ref[...] = k.astype(BF16)
    vb_ref[...] = v.astype(BF16)
    qi_ref[...] = proj(_C_QI, N_IDX_HEADS * IDX_DIM).astype(BF16)
    kw = proj(_C_KW, LANES)
    kw_ref[...] = kw
    ki_ref[...] = kw[:, :IDX_DIM]
    kk_ref[...] = proj(_C_KK, LANES).astype(BF16)
    xr_ref[...] = proj(_C_XR, rnn_width)
    gr_ref[...] = proj(_C_XR + rnn_width, rnn_width)


def _project(x2d, g, w_packed, rnn_width, tm):
    n, d = x2d.shape
    ncol = w_packed.shape[1]
    row = lambda w: pl.BlockSpec((tm, w), lambda i: (i, 0))
    out_shape = (
        jax.ShapeDtypeStruct((n, ATTN_WIDTH), BF16),
        jax.ShapeDtypeStruct((n, LANES), F32),
        jax.ShapeDtypeStruct((n, LANES), F32),
        jax.ShapeDtypeStruct((n, LANES), BF16),
        jax.ShapeDtypeStruct((n, LANES), BF16),
        jax.ShapeDtypeStruct((n, N_IDX_HEADS * IDX_DIM), BF16),
        jax.ShapeDtypeStruct((n, IDX_DIM), F32),
        jax.ShapeDtypeStruct((n, LANES), BF16),
        jax.ShapeDtypeStruct((n, LANES), F32),
        jax.ShapeDtypeStruct((n, rnn_width), F32),
        jax.ShapeDtypeStruct((n, rnn_width), F32),
    )
    out_specs = (row(ATTN_WIDTH), row(LANES), row(LANES), row(LANES), row(LANES),
                 row(N_IDX_HEADS * IDX_DIM), row(IDX_DIM), row(LANES), row(LANES),
                 row(rnn_width), row(rnn_width))
    return pl.pallas_call(
        functools.partial(_proj_kernel, rnn_width=rnn_width),
        grid=(n // tm,),
        in_specs=[row(d), pl.BlockSpec((1, d), lambda i: (0, 0)),
                  pl.BlockSpec((d, ncol), lambda i: (0, 0))],
        out_specs=out_specs,
        out_shape=out_shape,
        compiler_params=pltpu.CompilerParams(dimension_semantics=("arbitrary",),
                                             vmem_limit_bytes=VMEM_LIMIT_BYTES),
        name="proj",
    )(x2d, g, w_packed)


def _bucket_tables():
    half = NUM_BUCKETS // 2
    max_exact = half // 2
    r = np.arange(Q_TILE, dtype=np.int64)[None, :]
    c = np.arange(KEY_TILE, dtype=np.int64)[:, None]
    tabs = []
    for i in range(3):
        rel = (i - 2) * LANES + c - r
        side = np.where(rel > 0, half, 0)
        n = np.abs(rel)
        nf = np.maximum(n, 1).astype(np.float32)
        ratio = np.log(nf / np.float32(max_exact)) / np.float32(math.log(MAX_DISTANCE / max_exact))
        large = max_exact + (ratio * np.float32(half - max_exact)).astype(np.int32)
        large = np.minimum(large, half - 1)
        tabs.append(side + np.where(n < max_exact, n, large))
    return np.stack(tabs).astype(np.int32)


def _size_classes(total_rows):
    return tuple(range(SEL_CHUNK, total_rows + 1, SEL_CHUNK))


def _skip_subnormal_keys(k, min_normal):
    k = jnp.where((k > 0) & (k < min_normal), min_normal, k)
    return jnp.where((k < 0) & (k > -min_normal), 0, k)


def _key_to_float(k):
    k = _skip_subnormal_keys(k, F32_MIN_NORMAL_KEY)
    bits = jnp.where(k < 0, (-k) | INT_MIN, k)
    return lax.bitcast_convert_type(bits, F32)


def _attn_kernel(rb_ref, bkt_ref, q_ref, qi_ref, kw_ref, kk_ref, kb_ref, vt_ref, o_ref,
                 sc_ref, qit_ref, qt_ref, tbl_ref, s_ref, mx_ref, den_ref, acc_ref,
                 thr_ref, ngt_ref, neq_ref, tie_ref,
                 *, past, n_keys, top_k):
    tq, lt, ct = Q_TILE, KEY_TILE, SEL_CHUNK
    far_bucket = NUM_BUCKETS // 2 - 1
    j = pl.program_id(1)
    q0 = past + j * tq

    @pl.when((pl.program_id(0) == 0) & (j == 0))
    def _():
        for i in range(3):
            bkt = bkt_ref[i]
            for slot, h in enumerate(_HEAD_ORDER):
                t = jnp.zeros((lt, tq), F32)
                for b in range(NUM_BUCKETS):
                    t = jnp.where(bkt == b, rb_ref[b, h], t)
                tbl_ref[i, slot] = t - rb_ref[far_bucket, h]

    lo_rows = lax.broadcasted_iota(I32, (LANES, tq), 0) < HEAD_DIM
    for c in range(N_HEADS // 2):
        qi_t = qi_ref[:, c * LANES:(c + 1) * LANES].astype(F32).T
        q_t = q_ref[:, c * LANES:(c + 1) * LANES].astype(F32).T
        for half in range(2):
            h = 2 * c + half
            keep = lo_rows if half == 0 else jnp.logical_not(lo_rows)
            qit_ref[:, h * tq:(h + 1) * tq] = jnp.where(keep, qi_t, 0.0).astype(BF16)
            qt_ref[:, h * tq:(h + 1) * tq] = jnp.where(keep, q_t, 0.0).astype(BF16)

    wi_t = kw_ref[...].T[IDX_DIM:IDX_DIM + N_IDX_HEADS, :]
    wi_t = (wi_t * (N_IDX_HEADS ** -0.5)) * (IDX_DIM ** -0.5)

    qpos = q0 + lax.broadcasted_iota(I32, (1, tq), 1)
    n_adm = jnp.minimum(((qpos // CHUNK) + 1) * CHUNK, n_keys)
    n_adm_max = jnp.minimum(((q0 + tq - 1) // CHUNK + 1) * CHUNK, n_keys)
    n_kt = (n_adm_max + lt - 1) // lt
    n_ct = (n_adm_max + ct - 1) // ct
    crow = lax.broadcasted_iota(I32, (ct, tq), 0)

    def for_tiles(first, stop, span_fn):
        n = stop - first

        def longest(i, carry):
            span_fn(pl.multiple_of((first + MAX_SPAN * i) * lt, lt), MAX_SPAN)
            return carry

        lax.fori_loop(0, n // MAX_SPAN, longest, 0)
        tiles = MAX_SPAN // 2
        while tiles:
            done = (n // (2 * tiles)) * (2 * tiles)
            pl.when((n & tiles) != 0)(functools.partial(
                span_fn, pl.multiple_of((first + done) * lt, lt), tiles))
            tiles //= 2

    def score_span(base, tiles):
        rows = tiles * lt
        s = jnp.dot(kk_ref[pl.ds(base, rows), :], qit_ref[...], preferred_element_type=F32)
        acc = jnp.zeros((rows, tq), F32)
        for h in range(N_IDX_HEADS):
            acc = acc + wi_t[h:h + 1, :] * jnp.maximum(s[:, h * tq:(h + 1) * tq], 0.0)
        key_pos = lax.broadcasted_iota(I32, (rows, tq), 0) + base
        sc_ref[pl.ds(base, rows), :] = jnp.where(key_pos < n_adm, acc, -jnp.inf)

    for_tiles(0, n_kt, score_span)

    def select(rows):
        def blank(kt, carry):
            base = pl.multiple_of(kt * lt, lt)
            sc_ref[pl.ds(base, lt), :] = jnp.full((lt, tq), -jnp.inf, F32)
            return carry

        lax.fori_loop(n_kt, rows // lt, blank, 0)

        def count(cand):
            parts = [None] * SEL_ACCUMULATORS
            for r in range(rows // SUBLANES):
                hit = jnp.where(sc_ref[r * SUBLANES:(r + 1) * SUBLANES, :] >= cand, 1, 0)
                a = parts[r % SEL_ACCUMULATORS]
                parts[r % SEL_ACCUMULATORS] = hit if a is None else a + hit
            total = functools.reduce(lambda a, b: a + b, [a for a in parts if a is not None])
            return jnp.sum(total, axis=0, keepdims=True)

        def step(i, carry):
            t, n_ge = carry
            cand = t + (jnp.int32(1) << (31 - i))
            c = count(_key_to_float(cand))
            ok = c >= top_k
            return jnp.where(ok, cand, t), jnp.where(ok, c, n_ge)

        init = (jnp.full((1, tq), INT_MIN, I32), jnp.full((1, tq), rows, I32))
        t, n_ge = lax.fori_loop(0, 32, step, init)
        n_gt = count(_key_to_float(t + 1))
        thr_ref[...] = _key_to_float(t)
        ngt_ref[...] = n_gt
        neq_ref[...] = n_ge - n_gt

    classes = _size_classes(sc_ref.shape[0])
    for below, rows in zip((0,) + classes[:-1], classes):
        pl.when((n_adm_max > below) & (n_adm_max <= rows))(functools.partial(select, rows))

    thr = thr_ref[...]
    n_gt = ngt_ref[...]
    n_eq = neq_ref[...]

    def count32(pred):
        def body(c, cnt):
            base = pl.multiple_of(c * ct, ct)
            hit = pred(sc_ref[pl.ds(base, ct), :], base).astype(I32)
            return cnt + jnp.sum(hit.reshape(ct // SUBLANES, SUBLANES, tq), axis=0)
        cnt = lax.fori_loop(0, n_ct, body, jnp.zeros((SUBLANES, tq), I32))
        return jnp.sum(cnt, axis=0, keepdims=True)

    want = top_k - n_gt
    finite_thr = thr > -jnp.inf
    need = (n_eq > want) & finite_thr & (qpos < n_keys)
    tie_ref[...] = jnp.where(finite_thr, INT_MAX, -1)

    @pl.when(jnp.max(need.astype(I32)) > 0)
    def _():
        nbits = max(1, int(sc_ref.shape[0]).bit_length())

        def tie_step(i, m):
            cand = m | (jnp.int32(1) << (nbits - 1 - i))
            c = count32(lambda blk, base: (blk == thr) & (crow + base < cand))
            return jnp.where(c < want, cand, m)

        m = lax.fori_loop(0, nbits, tie_step, jnp.zeros((1, tq), I32))
        tie_ref[...] = jnp.where(need, m, tie_ref[...])

    tie_max = tie_ref[...]

    n_far = jnp.clip((q0 - (LANES - 1)) // lt, 0, n_kt)

    def logits_span(base, tiles, near=False):
        rows = tiles * lt
        blk = sc_ref[pl.ds(base, rows), :]
        key_pos = lax.broadcasted_iota(I32, (rows, tq), 0) + base
        sel = (blk > thr) | ((blk == thr) & (key_pos <= tie_max))
        mask = jnp.where(sel, 0.0, MASKED)
        s = jnp.dot(kb_ref[pl.ds(base, rows), :], qt_ref[...], preferred_element_type=F32)
        first_table = (base - q0) // LANES + 2
        tops = []
        for h in range(N_HEADS):
            sh = s[:, h * tq:(h + 1) * tq] + mask
            if near:
                sh = sh + jnp.concatenate([tbl_ref[first_table + 2 * t, h] for t in range(tiles)], axis=0)
            sh = sh * LOG2_E
            s_ref[pl.ds(base, rows), h * tq:(h + 1) * tq] = sh
            tops.append(jnp.max(sh.reshape(rows // SUBLANES, SUBLANES, tq), axis=0))
        mx_ref[...] = jnp.maximum(mx_ref[...], jnp.concatenate(tops, axis=1))

    mx_ref[...] = jnp.full(mx_ref.shape, MASKED * LOG2_E, F32)
    for_tiles(0, n_far, logits_span)
    for tiles in (1, 2):
        pl.when(n_kt - n_far == tiles)(functools.partial(
            logits_span, pl.multiple_of(n_far * lt, lt), tiles, near=True))
    m_fin = jnp.max(mx_ref[...], axis=0, keepdims=True)

    acc_ref[...] = jnp.zeros(acc_ref.shape, F32)
    den_ref[...] = jnp.zeros(den_ref.shape, F32)

    def pv_span(base, tiles):
        rows = tiles * lt
        p = jnp.exp2(s_ref[pl.ds(base, rows), :] - m_fin)
        kt = base // lt
        values_t = jnp.concatenate([vt_ref[kt + t] for t in range(tiles)], axis=1)
        acc_ref[...] += jnp.dot(values_t, p.astype(BF16), preferred_element_type=F32)
        den_ref[...] += jnp.sum(p.reshape(rows // SUBLANES, SUBLANES, N_HEADS * tq), axis=0)

    for_tiles(0, n_kt, pv_span)
    out_t = acc_ref[...] / jnp.sum(den_ref[...], axis=0, keepdims=True)
    lo_lanes = lax.broadcasted_iota(I32, (tq, LANES), 1) < HEAD_DIM
    for c in range(N_HEADS // 2):
        lo = out_t[:, (2 * c) * tq:(2 * c + 1) * tq].T
        hi = out_t[:, (2 * c + 1) * tq:(2 * c + 2) * tq].T
        o_ref[:, c * LANES:(c + 1) * LANES] = jnp.where(lo_lanes, lo, hi).astype(o_ref.dtype)


def _attention(rel_bias, q, qi, kw, kk, kb, vb, *, past, n_keys):
    b, tq_total, _ = q.shape
    lp = kk.shape[1]
    top_k = min(TOPK_MAX, n_keys // 4)
    assert tq_total % Q_TILE == 0 and lp % KEY_TILE == 0 and past % KEY_TILE == 0
    assert lp >= n_keys and top_k <= KEY_TILE
    n_kt = lp // KEY_TILE
    lc = _round_up(lp, SEL_CHUNK)
    vt = vb.reshape(b, n_kt, KEY_TILE, LANES).swapaxes(2, 3)
    bkt = jnp.asarray(_bucket_tables())
    qspec = lambda w: pl.BlockSpec((None, Q_TILE, w), lambda bi, j: (bi, j, 0))
    kspec = pl.BlockSpec((None, lp, LANES), lambda bi, j: (bi, 0, 0))
    return pl.pallas_call(
        functools.partial(_attn_kernel, past=past, n_keys=n_keys, top_k=top_k),
        grid=(b, tq_total // Q_TILE),
        in_specs=[pl.BlockSpec(memory_space=pltpu.SMEM),
                  pl.BlockSpec((3, KEY_TILE, Q_TILE), lambda bi, j: (0, 0, 0)),
                  qspec(ATTN_WIDTH), qspec(N_IDX_HEADS * IDX_DIM), qspec(LANES),
                  kspec, kspec,
                  pl.BlockSpec((None, n_kt, LANES, KEY_TILE), lambda bi, j: (bi, 0, 0, 0))],
        out_specs=qspec(ATTN_WIDTH),
        out_shape=jax.ShapeDtypeStruct((b, tq_total, ATTN_WIDTH), BF16),
        scratch_shapes=[
            pltpu.VMEM((lc, Q_TILE), F32),
            pltpu.VMEM((LANES, N_IDX_HEADS * Q_TILE), BF16),
            pltpu.VMEM((LANES, N_HEADS * Q_TILE), BF16),
            pltpu.VMEM((3, N_HEADS, KEY_TILE, Q_TILE), F32),
            pltpu.VMEM((lp, N_HEADS * Q_TILE), F32),
            pltpu.VMEM((SUBLANES, N_HEADS * Q_TILE), F32),
            pltpu.VMEM((SUBLANES, N_HEADS * Q_TILE), F32),
            pltpu.VMEM((LANES, N_HEADS * Q_TILE), F32),
            pltpu.VMEM((1, Q_TILE), F32),
            pltpu.VMEM((1, Q_TILE), I32),
            pltpu.VMEM((1, Q_TILE), I32),
            pltpu.VMEM((1, Q_TILE), I32),
        ],
        compiler_params=pltpu.CompilerParams(dimension_semantics=("arbitrary", "arbitrary"),
                                             vmem_limit_bytes=VMEM_LIMIT_BYTES),
        name="attn",
    )(rel_bias, bkt, q, qi, kw, kk, kb, vt)


def _softplus(x):
    return jnp.maximum(x, 0.0) + jnp.log1p(jnp.exp(-jnp.abs(x)))


def _rglru_kernel(xr_ref, gr_ref, h0_ref, c0_ref, cw_ref, cb_ref, wa_ref, ba_ref, wx_ref, bx_ref,
                  lam_ref, rnn_ref, hl_ref, xs_ref, a_ref, u_ref, hs_ref, hc_ref, *, tm):
    t = pl.program_id(1)
    pad = SUBLANES
    hist = RNN_CONV - 1

    @pl.when(t == 0)
    def _():
        xs_ref[0:pad, :] = jnp.zeros((pad, xs_ref.shape[-1]), F32)
        xs_ref[pad - hist:pad, :] = c0_ref[...]
        hc_ref[...] = jnp.broadcast_to(h0_ref[...], hc_ref.shape)

    xs_ref[pad:pad + tm, :] = xr_ref[...]
    groups = (tm // SUBLANES, SUBLANES, xs_ref.shape[-1])
    xe = xs_ref[...].reshape((groups[0] + 1,) + groups[1:])
    row = lax.broadcasted_iota(I32, groups, 1)
    xc = cb_ref[...]
    for jj in range(RNN_CONV):
        k = hist - jj
        if k:
            rolled = pltpu.roll(xe, k, 1)
            delayed = jnp.where(row < k, rolled[:-1], rolled[1:])
        else:
            delayed = xe[1:]
        xc = xc + cw_ref[jj:jj + 1, :] * delayed
    xc = xc.reshape(tm, -1)
    xs_ref[0:pad, :] = xs_ref[tm:tm + pad, :]

    xcb = xc.astype(BF16)
    rg = jax.nn.sigmoid(jnp.dot(xcb, wa_ref[...], preferred_element_type=F32) + ba_ref[...])
    ig = jax.nn.sigmoid(jnp.dot(xcb, wx_ref[...], preferred_element_type=F32) + bx_ref[...])
    log_a = (-RGLRU_C * rg) * _softplus(-lam_ref[...])
    a = jnp.exp(log_a)
    u = jnp.sqrt(-jnp.tanh(log_a) * (a * a + 1.0)) * (ig * xc)

    groups = (tm // SUBLANES, SUBLANES, a.shape[-1])
    a = a.reshape(groups)
    u = u.reshape(groups)
    row = lax.broadcasted_iota(I32, groups, 1)
    for s in (1, 2, 4):
        a_prev = pltpu.roll(a, s, 1)
        u_prev = pltpu.roll(u, s, 1)
        use = row >= s
        u = jnp.where(use, a * u_prev + u, u)
        a = jnp.where(use, a * a_prev, a)
    a_ref[...] = a.reshape(tm, -1)
    u_ref[...] = u.reshape(tm, -1)

    def carry_body(g, h):
        base = pl.multiple_of(g * SUBLANES, SUBLANES)
        hg = a_ref[pl.ds(base, SUBLANES), :] * h + u_ref[pl.ds(base, SUBLANES), :]
        hs_ref[pl.ds(base, SUBLANES), :] = hg
        return jnp.broadcast_to(hg[SUBLANES - 1:SUBLANES, :], hg.shape)

    h_end = lax.fori_loop(0, tm // SUBLANES, carry_body, hc_ref[...])
    hc_ref[...] = h_end
    rnn_ref[...] = (hs_ref[...] * jax.nn.gelu(gr_ref[...])).astype(rnn_ref.dtype)

    @pl.when(t == pl.num_programs(1) - 1)
    def _():
        hl_ref[...] = h_end[0:1, :]


def _rglru(xr, gr, h0, conv0, conv_w, conv_b, wa_bd, ba, wx_bd, bx, lam, tm):
    b, t, r = xr.shape
    assert t % tm == 0 and tm % SUBLANES == 0
    tile = pl.BlockSpec((None, tm, r), lambda bi, ti: (bi, ti, 0))
    per_b = lambda rows: pl.BlockSpec((None, rows, r), lambda bi, ti: (bi, 0, 0))
    full = lambda a: pl.BlockSpec(a.shape, lambda bi, ti: (0,) * a.ndim)
    return pl.pallas_call(
        functools.partial(_rglru_kernel, tm=tm),
        grid=(b, t // tm),
        in_specs=[tile, tile, per_b(1), per_b(RNN_CONV - 1), full(conv_w), full(conv_b),
                  full(wa_bd), full(ba), full(wx_bd), full(bx), full(lam)],
        out_specs=(tile, per_b(1)),
        out_shape=(jax.ShapeDtypeStruct((b, t, r), BF16), jax.ShapeDtypeStruct((b, 1, r), F32)),
        scratch_shapes=[pltpu.VMEM((tm + SUBLANES, r), F32), pltpu.VMEM((tm, r), F32),
                        pltpu.VMEM((tm, r), F32), pltpu.VMEM((tm, r), F32),
                        pltpu.VMEM((SUBLANES, r), F32)],
        compiler_params=pltpu.CompilerParams(dimension_semantics=("arbitrary", "arbitrary"),
                                             vmem_limit_bytes=VMEM_LIMIT_BYTES),
        name="rglru",
    )(xr, gr, h0, conv0, conv_w, conv_b, wa_bd, ba, wx_bd, bx, lam)


def _ffn_kernel(x_ref, at_ref, rn_ref, c0_ref, woa_ref, wor_ref, gf_ref, wu_ref, wg_ref, cw_ref,
                cb_ref, wd_ref, gl_ref, y_ref, fc_ref, up_ref, *, tm, final_norm):
    t = pl.program_id(1)
    pad = SUBLANES
    hist = FFN_CONV - 1

    @pl.when(t == 0)
    def _():
        up_ref[pad - hist:pad, :] = c0_ref[...]

    x1 = (x_ref[...]
          + jnp.dot(at_ref[...], woa_ref[...], preferred_element_type=F32)
          + jnp.dot(rn_ref[...], wor_ref[...], preferred_element_type=F32))
    f = _rmsnorm(x1, gf_ref[...]).astype(BF16)
    up_ref[pad:pad + tm, :] = jnp.dot(f, wu_ref[...], preferred_element_type=F32)
    up = cb_ref[...] + cw_ref[0:1, :] * up_ref[pad - hist:pad - hist + tm, :]
    for jj in range(1, FFN_CONV):
        up = up + cw_ref[jj:jj + 1, :] * up_ref[pad - hist + jj:pad - hist + jj + tm, :]
    tail = up_ref[pad + tm - hist:pad + tm, :]
    up_ref[pad - hist:pad, :] = tail
    gate = jnp.dot(f, wg_ref[...], preferred_element_type=F32)
    act = (jax.nn.gelu(up) * gate).astype(BF16)
    x2 = x1 + jnp.dot(act, wd_ref[...], preferred_element_type=F32)
    y_ref[...] = _rmsnorm(x2, gl_ref[...]) if final_norm else x2

    @pl.when(t == pl.num_programs(1) - 1)
    def _():
        fc_ref[...] = tail


def _ffn(x, attn, rnn, conv0, wo_a, wo_r, g_ffn, w_up, w_gate, conv_w, conv_b, w_down, g_last,
         tm, final_norm):
    b, t, d = x.shape
    dff = w_up.shape[1]
    assert t % tm == 0
    tile = lambda w: pl.BlockSpec((None, tm, w), lambda bi, ti: (bi, ti, 0))
    per_b = pl.BlockSpec((None, FFN_CONV - 1, dff), lambda bi, ti: (bi, 0, 0))
    full = lambda a: pl.BlockSpec(a.shape, lambda bi, ti: (0,) * a.ndim,
                                  pipeline_mode=pl.Buffered(1))
    return pl.pallas_call(
        functools.partial(_ffn_kernel, tm=tm, final_norm=final_norm),
        grid=(b, t // tm),
        in_specs=[tile(d), tile(attn.shape[-1]), tile(rnn.shape[-1]), per_b, full(wo_a), full(wo_r),
                  full(g_ffn), full(w_up), full(w_gate), full(conv_w), full(conv_b), full(w_down),
                  full(g_last)],
        out_specs=(tile(d), per_b),
        out_shape=(jax.ShapeDtypeStruct((b, t, d), F32),
                   jax.ShapeDtypeStruct((b, FFN_CONV - 1, dff), F32)),
        scratch_shapes=[pltpu.VMEM((tm + SUBLANES, dff), F32)],
        compiler_params=pltpu.CompilerParams(dimension_semantics=("arbitrary", "arbitrary"),
                                             vmem_limit_bytes=VMEM_LIMIT_BYTES),
        name="ffn",
    )(x, attn, rnn, conv0, wo_a, wo_r, g_ffn, w_up, w_gate, conv_w, conv_b, w_down, g_last)


_HEAD_ORDER = tuple(h for c in range(GQA_GROUP) for h in (c, GQA_GROUP + c))


def _pack_layer_weights(w_in, wa, wx, w_out, w_up, w_gate, w_down):
    d = w_in.shape[0]
    sizes = [ATTN_WIDTH, N_KV_HEADS * HEAD_DIM, N_KV_HEADS * HEAD_DIM, N_IDX_HEADS * IDX_DIM,
             IDX_DIM, N_IDX_HEADS]
    offs = np.cumsum([0] + sizes)
    wq, wk, wv, wqi, wki, wwi = (w_in[:, offs[i]:offs[i + 1]] for i in range(6))
    wrest = w_in[:, offs[6]:]
    rnn_width = wrest.shape[1] // 2
    order = np.asarray(_HEAD_ORDER)
    wq = wq.reshape(d, N_HEADS, HEAD_DIM)[:, order, :].reshape(d, ATTN_WIDTH)
    zpad = jnp.zeros((d, LANES - IDX_DIM - N_IDX_HEADS), w_in.dtype)
    packed = jnp.concatenate([wq, wk, wv, wqi, wki, wwi, zpad, wki, wki, wrest], axis=1).astype(BF16)
    eye = jnp.eye(RNN_BLOCKS, dtype=wa.dtype)
    bd = lambda w: jnp.einsum("nij,nm->nimj", w, eye).reshape(rnn_width, rnn_width).astype(BF16)
    wo_a = w_out[:ATTN_WIDTH].reshape(N_HEADS, HEAD_DIM, -1)[order].reshape(ATTN_WIDTH, -1)
    wo_r = w_out[ATTN_WIDTH:]
    return dict(w_packed=packed, rnn_width=rnn_width, wa_bd=bd(wa), wx_bd=bd(wx),
                wo_a=wo_a.astype(BF16), wo_r=wo_r.astype(BF16), w_up=w_up.astype(BF16),
                w_gate=w_gate.astype(BF16), w_down=w_down.astype(BF16))


def _pad_rows(a, rows):
    return a if a.shape[1] == rows else jnp.pad(a, ((0, 0), (0, rows - a.shape[1]), (0, 0)))


def _round_up(n, m):
    return -(-n // m) * m


def _layer(x, ck, cv, ckidx, h0, rconv0, fconv0, pw, norm_mix, conv_w, conv_b, ba, bx, lam,
           rel_bias, norm_ffn, fconv_w, fconv_b, norm_last, final_norm):
    b, t, d = x.shape
    past = ck.shape[1]
    r = pw["rnn_width"]
    n_tok = b * t
    tm = min(512, n_tok)
    (q, k, v, kb, vb, qi, ki, kk, kw, xr, gr) = _project(
        x.reshape(n_tok, d), norm_mix[None, :], pw["w_packed"], r, tm)
    b3 = lambda a: a.reshape(b, t, a.shape[-1])

    n_keys = past + t
    lp = _round_up(n_keys, KEY_TILE)
    tq = _round_up(t, Q_TILE)
    if past:
        ck2 = ck.reshape(b, past, -1).astype(BF16)
        cv2 = cv.reshape(b, past, -1).astype(BF16)
        cki = ckidx.astype(BF16)
        keys_k = jnp.concatenate([ck2, b3(kb)], axis=1)
        keys_v = jnp.concatenate([cv2, b3(vb)], axis=1)
        keys_i = jnp.concatenate([jnp.concatenate([cki, cki], axis=-1), b3(kk)], axis=1)
    else:
        keys_k, keys_v, keys_i = b3(kb), b3(vb), b3(kk)
    attn = _attention(rel_bias, _pad_rows(b3(q), tq), _pad_rows(b3(qi), tq), _pad_rows(b3(kw), tq),
                      _pad_rows(keys_i, lp), _pad_rows(keys_k, lp), _pad_rows(keys_v, lp),
                      past=past, n_keys=n_keys)[:, :t]

    tr = min(512, t)
    rnn, h_last = _rglru(b3(xr), b3(gr), h0[:, None, :], rconv0, conv_w, conv_b[None, :],
                         pw["wa_bd"], ba[None, :], pw["wx_bd"], bx[None, :], lam[None, :], tr)
    assert t >= RNN_CONV - 1 and t >= FFN_CONV - 1
    rconv_new = b3(xr)[:, t - (RNN_CONV - 1):, :]

    tf = min(512, t)
    y, fconv_new = _ffn(x, attn, rnn, fconv0, pw["wo_a"], pw["wo_r"], norm_ffn[None, :], pw["w_up"],
                        pw["w_gate"], fconv_w, fconv_b[None, :], pw["w_down"], norm_last[None, :],
                        tf, final_norm)
    k_new = k.reshape(b, t, N_KV_HEADS, HEAD_DIM)
    v_new = v.reshape(b, t, N_KV_HEADS, HEAD_DIM)
    return y, (k_new, v_new, b3(ki), h_last[:, 0, :], rconv_new, fconv_new)


def kernel(x_prompt, x_sample, cache_k, cache_v, cache_kidx, state_rglru_h, state_rglru_conv,
           state_ffn_conv, norm_mix, w_in, rglru_conv_w, rglru_conv_b, rglru_wa, rglru_ba,
           rglru_wx, rglru_bx, rglru_lambda, rel_bias, w_out, norm_ffn, w_ffn_up, w_ffn_gate,
           ffn_conv_w, ffn_conv_b, w_ffn_down, norm_final):
    depth = w_in.shape[0]
    bp = x_prompt.shape[0]
    dt = x_prompt.dtype
    xp, xs = x_prompt, x_sample
    outs_p, outs_s = [], []
    for i in range(depth):
        pw = _pack_layer_weights(w_in[i], rglru_wa[i], rglru_wx[i], w_out[i], w_ffn_up[i],
                                 w_ffn_gate[i], w_ffn_down[i])
        r = pw["rnn_width"]
        last = i == depth - 1
        lw = (pw, norm_mix[i], rglru_conv_w[i], rglru_conv_b[i], rglru_ba[i], rglru_bx[i],
              rglru_lambda[i], rel_bias, norm_ffn[i], ffn_conv_w[i], ffn_conv_b[i], norm_final, last)
        xp, st_p = _layer(xp,
                          jnp.zeros((bp, 0, N_KV_HEADS, HEAD_DIM), dt),
                          jnp.zeros((bp, 0, N_KV_HEADS, HEAD_DIM), dt),
                          jnp.zeros((bp, 0, IDX_DIM), dt),
                          jnp.zeros((bp, r), dt),
                          jnp.zeros((bp, RNN_CONV - 1, r), dt),
                          jnp.zeros((bp, FFN_CONV - 1, w_ffn_up.shape[-1]), dt),
                          *lw)
        xs, st_s = _layer(xs, cache_k[i], cache_v[i], cache_kidx[i], state_rglru_h[i],
                          state_rglru_conv[i], state_ffn_conv[i], *lw)
        outs_p.append(st_p)
        outs_s.append(st_s)
    stack = lambda outs, j: jnp.stack([o[j] for o in outs])
    return ((xp, xs) + tuple(stack(outs_p, j) for j in range(6))
            + tuple(stack(outs_s, j) for j in range(6)))
```

```python
import functools
import math

import jax
import jax.numpy as jnp
import numpy as np
from jax import lax
from jax.experimental import pallas as pl
from jax.experimental.pallas import tpu as pltpu

F32 = jnp.float32
BF16 = jnp.bfloat16
I32 = jnp.int32

CHUNK = 64
N_HEADS = 8
HEAD_DIM = 64
N_KV_HEADS = 2
GQA_GROUP = N_HEADS // N_KV_HEADS
ATTN_WIDTH = N_HEADS * HEAD_DIM
N_IDX_HEADS = 8
IDX_DIM = 64
TOPK_MAX = 256
NUM_BUCKETS = 32
MAX_DISTANCE = 128
RNN_BLOCKS = 8
RNN_CONV = 4
RGLRU_C = 8.0
FFN_CONV = 3
EPS = 1e-6

LANES = 128
SUBLANES = 8
VMEM_LIMIT_BYTES = 56 * 1024 * 1024

Q_TILE = 128
KEY_TILE = 256
MAX_SPAN = 8
SEL_CHUNK = Q_TILE
SEL_ACCUMULATORS = 8
MASKED = -1e30
INT_MIN = -(2 ** 31)
INT_MAX = 2 ** 31 - 1
F32_MIN_NORMAL_KEY = 2 ** 23
LOG2_E = 1.4426950408889634

_C_Q = 0
_C_K = _C_Q + ATTN_WIDTH
_C_V = _C_K + LANES
_C_QI = _C_V + LANES
_C_KW = _C_QI + N_IDX_HEADS * IDX_DIM
_C_KK = _C_KW + LANES
_C_XR = _C_KK + LANES


def _rmsnorm(x, g):
    y = x * lax.rsqrt(jnp.mean(x * x, axis=-1, keepdims=True) + EPS)
    return y * g


def _proj_kernel(x_ref, g_ref, w_ref, q_ref, k_ref, v_ref, kb_ref, vb_ref, qi_ref, ki_ref,
                 kk_ref, kw_ref, xr_ref, gr_ref, *, rnn_width):
    h = _rmsnorm(x_ref[...], g_ref[...]).astype(BF16)

    def proj(start, width):
        return jnp.dot(h, w_ref[:, start:start + width], preferred_element_type=F32)

    q_ref[...] = (proj(_C_Q, ATTN_WIDTH) * (HEAD_DIM ** -0.5)).astype(BF16)
    k = proj(_C_K, LANES)
    v = proj(_C_V, LANES)
    k_ref[...] = k
    v_ref[...] = v
    kb_ref[...] = k.astype(BF16)
    vb_ref[...] = v.astype(BF16)
    qi_ref[...] = proj(_C_QI, N_IDX_HEADS * IDX_DIM).astype(BF16)
    kw = proj(_C_KW, LANES)
    kw_ref[...] = kw
    ki_ref[...] = kw[:, :IDX_DIM]
    kk_ref[...] = proj(_C_KK, LANES).astype(BF16)
    xr_ref[...] = proj(_C_XR, rnn_width)
    gr_ref[...] = proj(_C_XR + rnn_width, rnn_width)


def _project(x2d, g, w_packed, rnn_width, tm):
    n, d = x2d.shape
    ncol = w_packed.shape[1]
    row = lambda w: pl.BlockSpec((tm, w), lambda i: (i, 0))
    out_shape = (
        jax.ShapeDtypeStruct((n, ATTN_WIDTH), BF16),
        jax.ShapeDtypeStruct((n, LANES), F32),
        jax.ShapeDtypeStruct((n, LANES), F32),
        jax.ShapeDtypeStruct((n, LANES), BF16),
        jax.ShapeDtypeStruct((n, LANES), BF16),
        jax.ShapeDtypeStruct((n, N_IDX_HEADS * IDX_DIM), BF16),
        jax.ShapeDtypeStruct((n, IDX_DIM), F32),
        jax.ShapeDtypeStruct((n, LANES), BF16),
        jax.ShapeDtypeStruct((n, LANES), F32),
        jax.ShapeDtypeStruct((n, rnn_width), F32),
        jax.ShapeDtypeStruct((n, rnn_width), F32),
    )
    out_specs = (row(ATTN_WIDTH), row(LANES), row(LANES), row(LANES), row(LANES),
                 row(N_IDX_HEADS * IDX_DIM), row(IDX_DIM), row(LANES), row(LANES),
                 row(rnn_width), row(rnn_width))
    return pl.pallas_call(
        functools.partial(_proj_kernel, rnn_width=rnn_width),
        grid=(n // tm,),
        in_specs=[row(d), pl.BlockSpec((1, d), lambda i: (0, 0)),
                  pl.BlockSpec((d, ncol), lambda i: (0, 0))],
        out_specs=out_specs,
        out_shape=out_shape,
        compiler_params=pltpu.CompilerParams(dimension_semantics=("arbitrary",),
                                             vmem_limit_bytes=VMEM_LIMIT_BYTES),
        name="proj",
    )(x2d, g, w_packed)


def _bucket_tables():
    half = NUM_BUCKETS // 2
    max_exact = half // 2
    r = np.arange(Q_TILE, dtype=np.int64)[None, :]
    c = np.arange(KEY_TILE, dtype=np.int64)[:, None]
    tabs = []
    for i in range(3):
        rel = (i - 2) * LANES + c - r
        side = np.where(rel > 0, half, 0)
        n = np.abs(rel)
        nf = np.maximum(n, 1).astype(np.float32)
        ratio = np.log(nf / np.float32(max_exact)) / np.float32(math.log(MAX_DISTANCE / max_exact))
        large = max_exact + (ratio * np.float32(half - max_exact)).astype(np.int32)
        large = np.minimum(large, half - 1)
        tabs.append(side + np.where(n < max_exact, n, large))
    return np.stack(tabs).astype(np.int32)


def _size_classes(total_rows, top_k):
    return tuple(range(_round_up(top_k, SEL_CHUNK), total_rows + 1, SEL_CHUNK))


def _skip_subnormal_keys(k, min_normal):
    k = jnp.where((k > 0) & (k < min_normal), min_normal, k)
    return jnp.where((k < 0) & (k > -min_normal), 0, k)


def _key_to_float(k):
    k = _skip_subnormal_keys(k, F32_MIN_NORMAL_KEY)
    bits = jnp.where(k < 0, (-k) | INT_MIN, k)
    return lax.bitcast_convert_type(bits, F32)


def _attn_kernel(rb_ref, bkt_ref, q_ref, qi_ref, kw_ref, kk_ref, kb_ref, vt_ref, o_ref,
                 sc_ref, qit_ref, qt_ref, tbl_ref, s_ref, mx_ref, den_ref, acc_ref,
                 thr_ref, ngt_ref, neq_ref, tie_ref,
                 *, past, n_keys, top_k):
    tq, lt, ct = Q_TILE, KEY_TILE, SEL_CHUNK
    far_bucket = NUM_BUCKETS // 2 - 1
    j = pl.program_id(1)
    q0 = past + j * tq

    @pl.when((pl.program_id(0) == 0) & (j == 0))
    def _():
        for i in range(3):
            bkt = bkt_ref[i]
            for slot, h in enumerate(_HEAD_ORDER):
                t = jnp.zeros((lt, tq), F32)
                for b in range(NUM_BUCKETS):
                    t = jnp.where(bkt == b, rb_ref[b, h], t)
                tbl_ref[i, slot] = t - rb_ref[far_bucket, h]

    lo_rows = lax.broadcasted_iota(I32, (LANES, tq), 0) < HEAD_DIM
    for c in range(N_HEADS // 2):
        qi_t = qi_ref[:, c * LANES:(c + 1) * LANES].astype(F32).T
        q_t = q_ref[:, c * LANES:(c + 1) * LANES].astype(F32).T
        for half in range(2):
            h = 2 * c + half
            keep = lo_rows if half == 0 else jnp.logical_not(lo_rows)
            qit_ref[:, h * tq:(h + 1) * tq] = jnp.where(keep, qi_t, 0.0).astype(BF16)
            qt_ref[:, h * tq:(h + 1) * tq] = jnp.where(keep, q_t, 0.0).astype(BF16)

    wi_t = kw_ref[...].T[IDX_DIM:IDX_DIM + N_IDX_HEADS, :]
    wi_t = (wi_t * (N_IDX_HEADS ** -0.5)) * (IDX_DIM ** -0.5)

    qpos = q0 + lax.broadcasted_iota(I32, (1, tq), 1)
    n_adm = jnp.minimum(((qpos // CHUNK) + 1) * CHUNK, n_keys)
    n_adm_max = jnp.minimum(((q0 + tq - 1) // CHUNK + 1) * CHUNK, n_keys)
    n_kt = (n_adm_max + lt - 1) // lt
    n_ct = (n_adm_max + ct - 1) // ct
    crow = lax.broadcasted_iota(I32, (ct, tq), 0)

    def for_tiles(first, stop, span_fn):
        n = stop - first
        span = MAX_SPAN
        while span * lt > kb_ref.shape[0]:
            span //= 2

        def longest(i, carry):
            span_fn(pl.multiple_of((first + span * i) * lt, lt), span)
            return carry

        lax.fori_loop(0, n // span, longest, 0)
        tiles = span // 2
        while tiles:
            done = (n // (2 * tiles)) * (2 * tiles)
            pl.when((n & tiles) != 0)(functools.partial(
                span_fn, pl.multiple_of((first + done) * lt, lt), tiles))
            tiles //= 2

    def score_span(base, tiles):
        rows = tiles * lt
        s = jnp.dot(kk_ref[pl.ds(base, rows), :], qit_ref[...], preferred_element_type=F32)
        acc = jnp.zeros((rows, tq), F32)
        for h in range(N_IDX_HEADS):
            acc = acc + wi_t[h:h + 1, :] * jnp.maximum(s[:, h * tq:(h + 1) * tq], 0.0)
        key_pos = lax.broadcasted_iota(I32, (rows, tq), 0) + base
        sc_ref[pl.ds(base, rows), :] = jnp.where(key_pos < n_adm, acc, -jnp.inf)

    for_tiles(0, n_kt, score_span)

    def select(rows):
        def blank(kt, carry):
            base = pl.multiple_of(kt * lt, lt)
            sc_ref[pl.ds(base, lt), :] = jnp.full((lt, tq), -jnp.inf, F32)
            return carry

        lax.fori_loop(n_kt, rows // lt, blank, 0)

        def count(cand):
            parts = [None] * SEL_ACCUMULATORS
            for r in range(rows // SUBLANES):
                hit = jnp.where(sc_ref[r * SUBLANES:(r + 1) * SUBLANES, :] >= cand, 1, 0)
                a = parts[r % SEL_ACCUMULATORS]
                parts[r % SEL_ACCUMULATORS] = hit if a is None else a + hit
            total = functools.reduce(lambda a, b: a + b, [a for a in parts if a is not None])
            return jnp.sum(total, axis=0, keepdims=True)

        def step(i, carry):
            t, n_ge = carry
            cand = t + (jnp.int32(1) << (31 - i))
            c = count(_key_to_float(cand))
            ok = c >= top_k
            return jnp.where(ok, cand, t), jnp.where(ok, c, n_ge)

        init = (jnp.full((1, tq), INT_MIN, I32), jnp.full((1, tq), rows, I32))
        t, n_ge = lax.fori_loop(0, 32, step, init)
        n_gt = count(_key_to_float(t + 1))
        thr_ref[...] = _key_to_float(t)
        ngt_ref[...] = n_gt
        neq_ref[...] = n_ge - n_gt

    classes = _size_classes(sc_ref.shape[0], top_k)
    for below, rows in zip((0,) + classes[:-1], classes):
        pl.when((n_adm_max > below) & (n_adm_max <= rows))(functools.partial(select, rows))

    thr = thr_ref[...]
    n_gt = ngt_ref[...]
    n_eq = neq_ref[...]

    def count32(pred):
        def body(c, cnt):
            base = pl.multiple_of(c * ct, ct)
            hit = pred(sc_ref[pl.ds(base, ct), :], base).astype(I32)
            return cnt + jnp.sum(hit.reshape(ct // SUBLANES, SUBLANES, tq), axis=0)
        cnt = lax.fori_loop(0, n_ct, body, jnp.zeros((SUBLANES, tq), I32))
        return jnp.sum(cnt, axis=0, keepdims=True)

    want = top_k - n_gt
    finite_thr = thr > -jnp.inf
    need = (n_eq > want) & finite_thr & (qpos < n_keys)
    tie_ref[...] = jnp.where(finite_thr, INT_MAX, -1)

    @pl.when(jnp.max(need.astype(I32)) > 0)
    def _():
        nbits = max(1, int(sc_ref.shape[0]).bit_length())

        def tie_step(i, m):
            cand = m | (jnp.int32(1) << (nbits - 1 - i))
            c = count32(lambda blk, base: (blk == thr) & (crow + base < cand))
            return jnp.where(c < want, cand, m)

        m = lax.fori_loop(0, nbits, tie_step, jnp.zeros((1, tq), I32))
        tie_ref[...] = jnp.where(need, m, tie_ref[...])

    tie_max = tie_ref[...]

    n_far = jnp.clip((q0 - (LANES - 1)) // lt, 0, n_kt)

    def logits_span(base, tiles, near=False):
        rows = tiles * lt
        blk = sc_ref[pl.ds(base, rows), :]
        key_pos = lax.broadcasted_iota(I32, (rows, tq), 0) + base
        sel = (blk > thr) | ((blk == thr) & (key_pos <= tie_max))
        mask = jnp.where(sel, 0.0, MASKED)
        s = jnp.dot(kb_ref[pl.ds(base, rows), :], qt_ref[...], preferred_element_type=F32)
        first_table = (base - q0) // LANES + 2
        tops = []
        for h in range(N_HEADS):
            sh = s[:, h * tq:(h + 1) * tq] + mask
            if near:
                sh = sh + jnp.concatenate([tbl_ref[first_table + 2 * t, h] for t in range(tiles)], axis=0)
            sh = sh * LOG2_E
            s_ref[pl.ds(base, rows), h * tq:(h + 1) * tq] = sh
            tops.append(jnp.max(sh.reshape(rows // SUBLANES, SUBLANES, tq), axis=0))
        mx_ref[...] = jnp.maximum(mx_ref[...], jnp.concatenate(tops, axis=1))

    mx_ref[...] = jnp.full(mx_ref.shape, MASKED * LOG2_E, F32)
    for_tiles(0, n_far, logits_span)
    for tiles in (1, 2):
        pl.when(n_kt - n_far == tiles)(functools.partial(
            logits_span, pl.multiple_of(n_far * lt, lt), tiles, near=True))
    m_fin = jnp.max(mx_ref[...], axis=0, keepdims=True)

    acc_ref[...] = jnp.zeros(acc_ref.shape, F32)
    den_ref[...] = jnp.zeros(den_ref.shape, F32)

    def pv_span(base, tiles):
        rows = tiles * lt
        p = jnp.exp2(s_ref[pl.ds(base, rows), :] - m_fin)
        kt = base // lt
        values_t = jnp.concatenate([vt_ref[kt + t] for t in range(tiles)], axis=1)
        acc_ref[...] += jnp.dot(values_t, p.astype(BF16), preferred_element_type=F32)
        den_ref[...] += jnp.sum(p.reshape(rows // SUBLANES, SUBLANES, N_HEADS * tq), axis=0)

    for_tiles(0, n_kt, pv_span)
    out_t = acc_ref[...] / jnp.sum(den_ref[...], axis=0, keepdims=True)
    lo_lanes = lax.broadcasted_iota(I32, (tq, LANES), 1) < HEAD_DIM
    for c in range(N_HEADS // 2):
        lo = out_t[:, (2 * c) * tq:(2 * c + 1) * tq].T
        hi = out_t[:, (2 * c + 1) * tq:(2 * c + 2) * tq].T
        o_ref[:, c * LANES:(c + 1) * LANES] = jnp.where(lo_lanes, lo, hi).astype(o_ref.dtype)


def _attention(rel_bias, q, qi, kw, kk, kb, vb, *, past, n_keys):
    b, tq_total, _ = q.shape
    lp = kk.shape[1]
    top_k = min(TOPK_MAX, n_keys // 4)
    assert tq_total % Q_TILE == 0 and lp % KEY_TILE == 0 and past % KEY_TILE == 0
    assert lp >= n_keys and top_k <= KEY_TILE
    n_kt = lp // KEY_TILE
    lc = _round_up(lp, SEL_CHUNK)
    vt = vb.reshape(b, n_kt, KEY_TILE, LANES).swapaxes(2, 3)
    bkt = jnp.asarray(_bucket_tables())
    qspec = lambda w: pl.BlockSpec((None, Q_TILE, w), lambda bi, j: (bi, j, 0))
    kspec = pl.BlockSpec((None, lp, LANES), lambda bi, j: (bi, 0, 0))
    return pl.pallas_call(
        functools.partial(_attn_kernel, past=past, n_keys=n_keys, top_k=top_k),
        grid=(b, tq_total // Q_TILE),
        in_specs=[pl.BlockSpec(memory_space=pltpu.SMEM),
                  pl.BlockSpec((3, KEY_TILE, Q_TILE), lambda bi, j: (0, 0, 0)),
                  qspec(ATTN_WIDTH), qspec(N_IDX_HEADS * IDX_DIM), qspec(LANES),
                  kspec, kspec,
                  pl.BlockSpec((None, n_kt, LANES, KEY_TILE), lambda bi, j: (bi, 0, 0, 0))],
        out_specs=qspec(ATTN_WIDTH),
        out_shape=jax.ShapeDtypeStruct((b, tq_total, ATTN_WIDTH), BF16),
        scratch_shapes=[
            pltpu.VMEM((lc, Q_TILE), F32),
            pltpu.VMEM((LANES, N_IDX_HEADS * Q_TILE), BF16),
            pltpu.VMEM((LANES, N_HEADS * Q_TILE), BF16),
            pltpu.VMEM((3, N_HEADS, KEY_TILE, Q_TILE), F32),
            pltpu.VMEM((lp, N_HEADS * Q_TILE), F32),
            pltpu.VMEM((SUBLANES, N_HEADS * Q_TILE), F32),
            pltpu.VMEM((SUBLANES, N_HEADS * Q_TILE), F32),
            pltpu.VMEM((LANES, N_HEADS * Q_TILE), F32),
            pltpu.VMEM((1, Q_TILE), F32),
            pltpu.VMEM((1, Q_TILE), I32),
            pltpu.VMEM((1, Q_TILE), I32),
            pltpu.VMEM((1, Q_TILE), I32),
        ],
        compiler_params=pltpu.CompilerParams(dimension_semantics=("arbitrary", "arbitrary"),
                                             vmem_limit_bytes=VMEM_LIMIT_BYTES),
        name="attn",
    )(rel_bias, bkt, q, qi, kw, kk, kb, vt)


def _softplus(x):
    return jnp.maximum(x, 0.0) + jnp.log1p(jnp.exp(-jnp.abs(x)))


def _rglru_kernel(xr_ref, gr_ref, h0_ref, c0_ref, cw_ref, cb_ref, wa_ref, ba_ref, wx_ref, bx_ref,
                  lam_ref, rnn_ref, hl_ref, xs_ref, a_ref, u_ref, hs_ref, hc_ref, *, tm):
    t = pl.program_id(1)
    pad = SUBLANES
    hist = RNN_CONV - 1

    @pl.when(t == 0)
    def _():
        xs_ref[0:pad, :] = jnp.zeros((pad, xs_ref.shape[-1]), F32)
        xs_ref[pad - hist:pad, :] = c0_ref[...]
        hc_ref[...] = jnp.broadcast_to(h0_ref[...], hc_ref.shape)

    xs_ref[pad:pad + tm, :] = xr_ref[...]
    groups = (tm // SUBLANES, SUBLANES, xs_ref.shape[-1])
    xe = xs_ref[...].reshape((groups[0] + 1,) + groups[1:])
    row = lax.broadcasted_iota(I32, groups, 1)
    xc = cb_ref[...]
    for jj in range(RNN_CONV):
        k = hist - jj
        if k:
            rolled = pltpu.roll(xe, k, 1)
            delayed = jnp.where(row < k, rolled[:-1], rolled[1:])
        else:
            delayed = xe[1:]
        xc = xc + cw_ref[jj:jj + 1, :] * delayed
    xc = xc.reshape(tm, -1)
    xs_ref[0:pad, :] = xs_ref[tm:tm + pad, :]

    xcb = xc.astype(BF16)
    rg = jax.nn.sigmoid(jnp.dot(xcb, wa_ref[...], preferred_element_type=F32) + ba_ref[...])
    ig = jax.nn.sigmoid(jnp.dot(xcb, wx_ref[...], preferred_element_type=F32) + bx_ref[...])
    log_a = (-RGLRU_C * rg) * _softplus(-lam_ref[...])
    a = jnp.exp(log_a)
    u = jnp.sqrt(-jnp.tanh(log_a) * (a * a + 1.0)) * (ig * xc)

    groups = (tm // SUBLANES, SUBLANES, a.shape[-1])
    a = a.reshape(groups)
    u = u.reshape(groups)
    row = lax.broadcasted_iota(I32, groups, 1)
    for s in (1, 2, 4):
        a_prev = pltpu.roll(a, s, 1)
        u_prev = pltpu.roll(u, s, 1)
        use = row >= s
        u = jnp.where(use, a * u_prev + u, u)
        a = jnp.where(use, a * a_prev, a)
    a_ref[...] = a.reshape(tm, -1)
    u_ref[...] = u.reshape(tm, -1)

    def carry_body(g, h):
        base = pl.multiple_of(g * SUBLANES, SUBLANES)
        hg = a_ref[pl.ds(base, SUBLANES), :] * h + u_ref[pl.ds(base, SUBLANES), :]
        hs_ref[pl.ds(base, SUBLANES), :] = hg
        return jnp.broadcast_to(hg[SUBLANES - 1:SUBLANES, :], hg.shape)

    h_end = lax.fori_loop(0, tm // SUBLANES, carry_body, hc_ref[...])
    hc_ref[...] = h_end
    rnn_ref[...] = (hs_ref[...] * jax.nn.gelu(gr_ref[...])).astype(rnn_ref.dtype)

    @pl.when(t == pl.num_programs(1) - 1)
    def _():
        hl_ref[...] = h_end[0:1, :]


def _rglru(xr, gr, h0, conv0, conv_w, conv_b, wa_bd, ba, wx_bd, bx, lam, tm):
    b, t, r = xr.shape
    assert t % tm == 0 and tm % SUBLANES == 0
    tile = pl.BlockSpec((None, tm, r), lambda bi, ti: (bi, ti, 0))
    per_b = lambda rows: pl.BlockSpec((None, rows, r), lambda bi, ti: (bi, 0, 0))
    full = lambda a: pl.BlockSpec(a.shape, lambda bi, ti: (0,) * a.ndim)
    return pl.pallas_call(
        functools.partial(_rglru_kernel, tm=tm),
        grid=(b, t // tm),
        in_specs=[tile, tile, per_b(1), per_b(RNN_CONV - 1), full(conv_w), full(conv_b),
                  full(wa_bd), full(ba), full(wx_bd), full(bx), full(lam)],
        out_specs=(tile, per_b(1)),
        out_shape=(jax.ShapeDtypeStruct((b, t, r), BF16), jax.ShapeDtypeStruct((b, 1, r), F32)),
        scratch_shapes=[pltpu.VMEM((tm + SUBLANES, r), F32), pltpu.VMEM((tm, r), F32),
                        pltpu.VMEM((tm, r), F32), pltpu.VMEM((tm, r), F32),
                        pltpu.VMEM((SUBLANES, r), F32)],
        compiler_params=pltpu.CompilerParams(dimension_semantics=("arbitrary", "arbitrary"),
                                             vmem_limit_bytes=VMEM_LIMIT_BYTES),
        name="rglru",
    )(xr, gr, h0, conv0, conv_w, conv_b, wa_bd, ba, wx_bd, bx, lam)


def _ffn_kernel(x_ref, at_ref, rn_ref, c0_ref, woa_ref, wor_ref, gf_ref, wu_ref, wg_ref, cw_ref,
                cb_ref, wd_ref, gl_ref, y_ref, fc_ref, up_ref, *, tm, final_norm):
    t = pl.program_id(1)
    pad = SUBLANES
    hist = FFN_CONV - 1

    @pl.when(t == 0)
    def _():
        up_ref[pad - hist:pad, :] = c0_ref[...]

    x1 = (x_ref[...]
          + jnp.dot(at_ref[...], woa_ref[...], preferred_element_type=F32)
          + jnp.dot(rn_ref[...], wor_ref[...], preferred_element_type=F32))
    f = _rmsnorm(x1, gf_ref[...]).astype(BF16)
    up_ref[pad:pad + tm, :] = jnp.dot(f, wu_ref[...], preferred_element_type=F32)
    up = cb_ref[...] + cw_ref[0:1, :] * up_ref[pad - hist:pad - hist + tm, :]
    for jj in range(1, FFN_CONV):
        up = up + cw_ref[jj:jj + 1, :] * up_ref[pad - hist + jj:pad - hist + jj + tm, :]
    tail = up_ref[pad + tm - hist:pad + tm, :]
    up_ref[pad - hist:pad, :] = tail
    gate = jnp.dot(f, wg_ref[...], preferred_element_type=F32)
    act = (jax.nn.gelu(up) * gate).astype(BF16)
    x2 = x1 + jnp.dot(act, wd_ref[...], preferred_element_type=F32)
    y_ref[...] = _rmsnorm(x2, gl_ref[...]) if final_norm else x2

    @pl.when(t == pl.num_programs(1) - 1)
    def _():
        fc_ref[...] = tail


def _ffn(x, attn, rnn, conv0, wo_a, wo_r, g_ffn, w_up, w_gate, conv_w, conv_b, w_down, g_last,
         tm, final_norm):
    b, t, d = x.shape
    dff = w_up.shape[1]
    assert t % tm == 0
    tile = lambda w: pl.BlockSpec((None, tm, w), lambda bi, ti: (bi, ti, 0))
    per_b = pl.BlockSpec((None, FFN_CONV - 1, dff), lambda bi, ti: (bi, 0, 0))
    full = lambda a: pl.BlockSpec(a.shape, lambda bi, ti: (0,) * a.ndim,
                                  pipeline_mode=pl.Buffered(1))
    return pl.pallas_call(
        functools.partial(_ffn_kernel, tm=tm, final_norm=final_norm),
        grid=(b, t // tm),
        in_specs=[tile(d), tile(attn.shape[-1]), tile(rnn.shape[-1]), per_b, full(wo_a), full(wo_r),
                  full(g_ffn), full(w_up), full(w_gate), full(conv_w), full(conv_b), full(w_down),
                  full(g_last)],
        out_specs=(tile(d), per_b),
        out_shape=(jax.ShapeDtypeStruct((b, t, d), F32),
                   jax.ShapeDtypeStruct((b, FFN_CONV - 1, dff), F32)),
        scratch_shapes=[pltpu.VMEM((tm + SUBLANES, dff), F32)],
        compiler_params=pltpu.CompilerParams(dimension_semantics=("arbitrary", "arbitrary"),
                                             vmem_limit_bytes=VMEM_LIMIT_BYTES),
        name="ffn",
    )(x, attn, rnn, conv0, wo_a, wo_r, g_ffn, w_up, w_gate, conv_w, conv_b, w_down, g_last)


_HEAD_ORDER = tuple(h for c in range(GQA_GROUP) for h in (c, GQA_GROUP + c))


def _pack_layer_weights(w_in, wa, wx, w_out, w_up, w_gate, w_down):
    d = w_in.shape[0]
    sizes = [ATTN_WIDTH, N_KV_HEADS * HEAD_DIM, N_KV_HEADS * HEAD_DIM, N_IDX_HEADS * IDX_DIM,
             IDX_DIM, N_IDX_HEADS]
    offs = np.cumsum([0] + sizes)
    wq, wk, wv, wqi, wki, wwi = (w_in[:, offs[i]:offs[i + 1]] for i in range(6))
    wrest = w_in[:, offs[6]:]
    rnn_width = wrest.shape[1] // 2
    order = np.asarray(_HEAD_ORDER)
    wq = wq.reshape(d, N_HEADS, HEAD_DIM)[:, order, :].reshape(d, ATTN_WIDTH)
    zpad = jnp.zeros((d, LANES - IDX_DIM - N_IDX_HEADS), w_in.dtype)
    packed = jnp.concatenate([wq, wk, wv, wqi, wki, wwi, zpad, wki, wki, wrest], axis=1).astype(BF16)
    eye = jnp.eye(RNN_BLOCKS, dtype=wa.dtype)
    bd = lambda w: jnp.einsum("nij,nm->nimj", w, eye).reshape(rnn_width, rnn_width).astype(BF16)
    wo_a = w_out[:ATTN_WIDTH].reshape(N_HEADS, HEAD_DIM, -1)[order].reshape(ATTN_WIDTH, -1)
    wo_r = w_out[ATTN_WIDTH:]
    return dict(w_packed=packed, rnn_width=rnn_width, wa_bd=bd(wa), wx_bd=bd(wx),
                wo_a=wo_a.astype(BF16), wo_r=wo_r.astype(BF16), w_up=w_up.astype(BF16),
                w_gate=w_gate.astype(BF16), w_down=w_down.astype(BF16))


def _pad_rows(a, rows):
    return a if a.shape[1] == rows else jnp.pad(a, ((0, 0), (0, rows - a.shape[1]), (0, 0)))


def _round_up(n, m):
    return -(-n // m) * m


def _layer(x, ck, cv, ckidx, h0, rconv0, fconv0, pw, norm_mix, conv_w, conv_b, ba, bx, lam,
           rel_bias, norm_ffn, fconv_w, fconv_b, norm_last, final_norm):
    b, t, d = x.shape
    past = ck.shape[1]
    r = pw["rnn_width"]
    n_tok = b * t
    tm = min(512, n_tok)
    (q, k, v, kb, vb, qi, ki, kk, kw, xr, gr) = _project(
        x.reshape(n_tok, d), norm_mix[None, :], pw["w_packed"], r, tm)
    b3 = lambda a: a.reshape(b, t, a.shape[-1])

    n_keys = past + t
    lp = _round_up(n_keys, KEY_TILE)
    tq = _round_up(t, Q_TILE)
    if past:
        ck2 = ck.reshape(b, past, -1).astype(BF16)
        cv2 = cv.reshape(b, past, -1).astype(BF16)
        cki = ckidx.astype(BF16)
        keys_k = jnp.concatenate([ck2, b3(kb)], axis=1)
        keys_v = jnp.concatenate([cv2, b3(vb)], axis=1)
        keys_i = jnp.concatenate([jnp.concatenate([cki, cki], axis=-1), b3(kk)], axis=1)
    else:
        keys_k, keys_v, keys_i = b3(kb), b3(vb), b3(kk)
    attn = _attention(rel_bias, _pad_rows(b3(q), tq), _pad_rows(b3(qi), tq), _pad_rows(b3(kw), tq),
                      _pad_rows(keys_i, lp), _pad_rows(keys_k, lp), _pad_rows(keys_v, lp),
                      past=past, n_keys=n_keys)[:, :t]

    tr = min(512, t)
    rnn, h_last = _rglru(b3(xr), b3(gr), h0[:, None, :], rconv0, conv_w, conv_b[None, :],
                         pw["wa_bd"], ba[None, :], pw["wx_bd"], bx[None, :], lam[None, :], tr)
    assert t >= RNN_CONV - 1 and t >= FFN_CONV - 1
    rconv_new = b3(xr)[:, t - (RNN_CONV - 1):, :]

    tf = min(512, t)
    y, fconv_new = _ffn(x, attn, rnn, fconv0, pw["wo_a"], pw["wo_r"], norm_ffn[None, :], pw["w_up"],
                        pw["w_gate"], fconv_w, fconv_b[None, :], pw["w_down"], norm_last[None, :],
                        tf, final_norm)
    k_new = k.reshape(b, t, N_KV_HEADS, HEAD_DIM)
    v_new = v.reshape(b, t, N_KV_HEADS, HEAD_DIM)
    return y, (k_new, v_new, b3(ki), h_last[:, 0, :], rconv_new, fconv_new)


def kernel(x_prompt, x_sample, cache_k, cache_v, cache_kidx, state_rglru_h, state_rglru_conv,
           state_ffn_conv, norm_mix, w_in, rglru_conv_w, rglru_conv_b, rglru_wa, rglru_ba,
           rglru_wx, rglru_bx, rglru_lambda, rel_bias, w_out, norm_ffn, w_ffn_up, w_ffn_gate,
           ffn_conv_w, ffn_conv_b, w_ffn_down, norm_final):
    depth = w_in.shape[0]
    bp = x_prompt.shape[0]
    dt = x_prompt.dtype
    xp, xs = x_prompt, x_sample
    outs_p, outs_s = [], []
    for i in range(depth):
        pw = _pack_layer_weights(w_in[i], rglru_wa[i], rglru_wx[i], w_out[i], w_ffn_up[i],
                                 w_ffn_gate[i], w_ffn_down[i])
        r = pw["rnn_width"]
        last = i == depth - 1
        lw = (pw, norm_mix[i], rglru_conv_w[i], rglru_conv_b[i], rglru_ba[i], rglru_bx[i],
              rglru_lambda[i], rel_bias, norm_ffn[i], ffn_conv_w[i], ffn_conv_b[i], norm_final, last)
        xp, st_p = _layer(xp,
                          jnp.zeros((bp, 0, N_KV_HEADS, HEAD_DIM), dt),
                          jnp.zeros((bp, 0, N_KV_HEADS, HEAD_DIM), dt),
                          jnp.zeros((bp, 0, IDX_DIM), dt),
                          jnp.zeros((bp, r), dt),
                          jnp.zeros((bp, RNN_CONV - 1, r), dt),
                          jnp.zeros((bp, FFN_CONV - 1, w_ffn_up.shape[-1]), dt),
                          *lw)
        xs, st_s = _layer(xs, cache_k[i], cache_v[i], cache_kidx[i], state_rglru_h[i],
                          state_rglru_conv[i], state_ffn_conv[i], *lw)
        outs_p.append(st_p)
        outs_s.append(st_s)
    stack = lambda outs, j: jnp.stack([o[j] for o in outs])
    return ((xp, xs) + tuple(stack(outs_p, j) for j in range(6))
            + tuple(stack(outs_s, j) for j in range(6)))
```

```python
import functools
import math

import jax
import jax.numpy as jnp
import numpy as np
from jax import lax
from jax.experimental import pallas as pl
from jax.experimental.pallas import tpu as pltpu

F32 = jnp.float32
BF16 = jnp.bfloat16
I32 = jnp.int32

CHUNK = 64
N_HEADS = 8
HEAD_DIM = 64
N_KV_HEADS = 2
GQA_GROUP = N_HEADS // N_KV_HEADS
ATTN_WIDTH = N_HEADS * HEAD_DIM
N_IDX_HEADS = 8
IDX_DIM = 64
TOPK_MAX = 256
NUM_BUCKETS = 32
MAX_DISTANCE = 128
RNN_BLOCKS = 8
RNN_CONV = 4
RGLRU_C = 8.0
FFN_CONV = 3
EPS = 1e-6

LANES = 128
SUBLANES = 8
VMEM_LIMIT_BYTES = 56 * 1024 * 1024

Q_TILE = 128
KEY_TILE = 256
MAX_SPAN = 8
SEL_CHUNK = Q_TILE
SEL_ACCUMULATORS = 8
MASKED = -1e30
INT_MIN = -(2 ** 31)
INT_MAX = 2 ** 31 - 1
F32_MIN_NORMAL_KEY = 2 ** 23
LOG2_E = 1.4426950408889634

_C_Q = 0
_C_K = _C_Q + ATTN_WIDTH
_C_V = _C_K + LANES
_C_QI = _C_V + LANES
_C_KW = _C_QI + N_IDX_HEADS * IDX_DIM
_C_KK = _C_KW + LANES
_C_XR = _C_KK + LANES


def _rmsnorm(x, g):
    y = x * lax.rsqrt(jnp.mean(x * x, axis=-1, keepdims=True) + EPS)
    return y * g


def _proj_kernel(x_ref, g_ref, w_ref, q_ref, k_ref, v_ref, kb_ref, vb_ref, qi_ref, ki_ref,
                 kk_ref, kw_ref, xr_ref, gr_ref, *, rnn_width):
    h = _rmsnorm(x_ref[...], g_ref[...]).astype(BF16)

    def proj(start, width):
        return jnp.dot(h, w_ref[:, start:start + width], preferred_element_type=F32)

    q_ref[...] = (proj(_C_Q, ATTN_WIDTH) * (HEAD_DIM ** -0.5)).astype(BF16)
    k = proj(_C_K, LANES)
    v = proj(_C_V, LANES)
    k_ref[...] = k
    v_ref[...] = v
    kb_ref[...] = k.astype(BF16)
    vb_ref[...] = v.astype(BF16)
    qi_ref[...] = proj(_C_QI, N_IDX_HEADS * IDX_DIM).astype(BF16)
    kw = proj(_C_KW, LANES)
    kw_ref[...] = kw
    ki_ref[...] = kw[:, :IDX_DIM]
    kk_ref[...] = proj(_C_KK, LANES).astype(BF16)
    xr_ref[...] = proj(_C_XR, rnn_width)
    gr_ref[...] = proj(_C_XR + rnn_width, rnn_width)


def _project(x2d, g, w_packed, rnn_width, tm):
    n, d = x2d.shape
    ncol = w_packed.shape[1]
    row = lambda w: pl.BlockSpec((tm, w), lambda i: (i, 0))
    out_shape = (
        jax.ShapeDtypeStruct((n, ATTN_WIDTH), BF16),
        jax.ShapeDtypeStruct((n, LANES), F32),
        jax.ShapeDtypeStruct((n, LANES), F32),
        jax.ShapeDtypeStruct((n, LANES), BF16),
        jax.ShapeDtypeStruct((n, LANES), BF16),
        jax.ShapeDtypeStruct((n, N_IDX_HEADS * IDX_DIM), BF16),
        jax.ShapeDtypeStruct((n, IDX_DIM), F32),
        jax.ShapeDtypeStruct((n, LANES), BF16),
        jax.ShapeDtypeStruct((n, LANES), F32),
        jax.ShapeDtypeStruct((n, rnn_width), F32),
        jax.ShapeDtypeStruct((n, rnn_width), F32),
    )
    out_specs = (row(ATTN_WIDTH), row(LANES), row(LANES), row(LANES), row(LANES),
                 row(N_IDX_HEADS * IDX_DIM), row(IDX_DIM), row(LANES), row(LANES),
                 row(rnn_width), row(rnn_width))
    return pl.pallas_call(
        functools.partial(_proj_kernel, rnn_width=rnn_width),
        grid=(n // tm,),
        in_specs=[row(d), pl.BlockSpec((1, d), lambda i: (0, 0)),
                  pl.BlockSpec((d, ncol), lambda i: (0, 0))],
        out_specs=out_specs,
        out_shape=out_shape,
        compiler_params=pltpu.CompilerParams(dimension_semantics=("arbitrary",),
                                             vmem_limit_bytes=VMEM_LIMIT_BYTES),
        name="proj",
    )(x2d, g, w_packed)


def _bucket_tables():
    half = NUM_BUCKETS // 2
    max_exact = half // 2
    r = np.arange(Q_TILE, dtype=np.int64)[None, :]
    c = np.arange(KEY_TILE, dtype=np.int64)[:, None]
    tabs = []
    for i in range(3):
        rel = (i - 2) * LANES + c - r
        side = np.where(rel > 0, half, 0)
        n = np.abs(rel)
        nf = np.maximum(n, 1).astype(np.float32)
        ratio = np.log(nf / np.float32(max_exact)) / np.float32(math.log(MAX_DISTANCE / max_exact))
        large = max_exact + (ratio * np.float32(half - max_exact)).astype(np.int32)
        large = np.minimum(large, half - 1)
        tabs.append(side + np.where(n < max_exact, n, large))
    return np.stack(tabs).astype(np.int32)


def _size_classes(total_rows, top_k):
    return tuple(range(_round_up(top_k, SEL_CHUNK), total_rows + 1, SEL_CHUNK))


def _skip_subnormal_keys(k, min_normal):
    k = jnp.where((k > 0) & (k < min_normal), min_normal, k)
    return jnp.where((k < 0) & (k > -min_normal), 0, k)


def _key_to_float(k):
    k = _skip_subnormal_keys(k, F32_MIN_NORMAL_KEY)
    bits = jnp.where(k < 0, (-k) | INT_MIN, k)
    return lax.bitcast_convert_type(bits, F32)


def _attn_kernel(rb_ref, bkt_ref, q_ref, qi_ref, kw_ref, kk_ref, kb_ref, vt_ref, o_ref,
                 sc_ref, qit_ref, qt_ref, tbl_ref, s_ref, mx_ref, den_ref, acc_ref,
                 thr_ref, ngt_ref, neq_ref, tie_ref,
                 *, past, n_keys, top_k):
    tq, lt, ct = Q_TILE, KEY_TILE, SEL_CHUNK
    far_bucket = NUM_BUCKETS // 2 - 1
    j = pl.program_id(1)
    q0 = past + j * tq

    @pl.when((pl.program_id(0) == 0) & (j == 0))
    def _():
        for i in range(3):
            bkt = bkt_ref[i]
            for slot, h in enumerate(_HEAD_ORDER):
                t = jnp.zeros((lt, tq), F32)
                for b in range(NUM_BUCKETS):
                    t = jnp.where(bkt == b, rb_ref[b, h], t)
                tbl_ref[i, slot] = t - rb_ref[far_bucket, h]

    lo_rows = lax.broadcasted_iota(I32, (LANES, tq), 0) < HEAD_DIM
    for c in range(N_HEADS // 2):
        qi_t = qi_ref[:, c * LANES:(c + 1) * LANES].astype(F32).T
        q_t = q_ref[:, c * LANES:(c + 1) * LANES].astype(F32).T
        for half in range(2):
            h = 2 * c + half
            keep = lo_rows if half == 0 else jnp.logical_not(lo_rows)
            qit_ref[:, h * tq:(h + 1) * tq] = jnp.where(keep, qi_t, 0.0).astype(BF16)
            qt_ref[:, h * tq:(h + 1) * tq] = jnp.where(keep, q_t, 0.0).astype(BF16)

    wi_t = kw_ref[...].T[IDX_DIM:IDX_DIM + N_IDX_HEADS, :]
    wi_t = (wi_t * (N_IDX_HEADS ** -0.5)) * (IDX_DIM ** -0.5)

    qpos = q0 + lax.broadcasted_iota(I32, (1, tq), 1)
    n_adm = jnp.minimum(((qpos // CHUNK) + 1) * CHUNK, n_keys)
    n_adm_max = jnp.minimum(((q0 + tq - 1) // CHUNK + 1) * CHUNK, n_keys)
    n_kt = (n_adm_max + lt - 1) // lt
    n_ct = (n_adm_max + ct - 1) // ct
    crow = lax.broadcasted_iota(I32, (ct, tq), 0)

    def for_tiles(first, stop, span_fn):
        n = stop - first
        span = MAX_SPAN
        while span * lt > kb_ref.shape[0]:
            span //= 2

        def longest(i, carry):
            span_fn(pl.multiple_of((first + span * i) * lt, lt), span)
            return carry

        lax.fori_loop(0, n // span, longest, 0)
        tiles = span // 2
        while tiles:
            done = (n // (2 * tiles)) * (2 * tiles)
            pl.when((n & tiles) != 0)(functools.partial(
                span_fn, pl.multiple_of((first + done) * lt, lt), tiles))
            tiles //= 2

    def score_span(base, tiles):
        rows = tiles * lt
        s = jnp.dot(kk_ref[pl.ds(base, rows), :], qit_ref[...], preferred_element_type=F32)
        acc = jnp.zeros((rows, tq), F32)
        for h in range(N_IDX_HEADS):
            acc = acc + wi_t[h:h + 1, :] * jnp.maximum(s[:, h * tq:(h + 1) * tq], 0.0)
        key_pos = lax.broadcasted_iota(I32, (rows, tq), 0) + base
        sc_ref[pl.ds(base, rows), :] = jnp.where(key_pos < n_adm, acc, -jnp.inf)

    for_tiles(0, n_kt, score_span)

    def select(rows):
        def blank(kt, carry):
            base = pl.multiple_of(kt * lt, lt)
            sc_ref[pl.ds(base, lt), :] = jnp.full((lt, tq), -jnp.inf, F32)
            return carry

        lax.fori_loop(n_kt, rows // lt, blank, 0)

        def count(cand):
            parts = [None] * SEL_ACCUMULATORS
            for r in range(rows // SUBLANES):
                hit = jnp.where(sc_ref[r * SUBLANES:(r + 1) * SUBLANES, :] >= cand, 1, 0)
                a = parts[r % SEL_ACCUMULATORS]
                parts[r % SEL_ACCUMULATORS] = hit if a is None else a + hit
            total = functools.reduce(lambda a, b: a + b, [a for a in parts if a is not None])
            return jnp.sum(total, axis=0, keepdims=True)

        def step(i, carry):
            t, n_ge = carry
            cand = t + (jnp.int32(1) << (31 - i))
            c = count(_key_to_float(cand))
            ok = c >= top_k
            return jnp.where(ok, cand, t), jnp.where(ok, c, n_ge)

        init = (jnp.full((1, tq), INT_MIN, I32), jnp.full((1, tq), rows, I32))
        t, n_ge = lax.fori_loop(0, 32, step, init)
        n_gt = count(_key_to_float(t + 1))
        thr_ref[...] = _key_to_float(t)
        ngt_ref[...] = n_gt
        neq_ref[...] = n_ge - n_gt

    classes = _size_classes(sc_ref.shape[0], top_k)
    for below, rows in zip((0,) + classes[:-1], classes):
        pl.when((n_adm_max > below) & (n_adm_max <= rows))(functools.partial(select, rows))

    thr = thr_ref[...]
    n_gt = ngt_ref[...]
    n_eq = neq_ref[...]

    def count32(pred):
        def body(c, cnt):
            base = pl.multiple_of(c * ct, ct)
            hit = pred(sc_ref[pl.ds(base, ct), :], base).astype(I32)
            return cnt + jnp.sum(hit.reshape(ct // SUBLANES, SUBLANES, tq), axis=0)
        cnt = lax.fori_loop(0, n_ct, body, jnp.zeros((SUBLANES, tq), I32))
        return jnp.sum(cnt, axis=0, keepdims=True)

    want = top_k - n_gt
    finite_thr = thr > -jnp.inf
    need = (n_eq > want) & finite_thr & (qpos < n_keys)
    tie_ref[...] = jnp.where(finite_thr, INT_MAX, -1)

    @pl.when(jnp.max(need.astype(I32)) > 0)
    def _():
        nbits = max(1, int(sc_ref.shape[0]).bit_length())

        def tie_step(i, m):
            cand = m | (jnp.int32(1) << (nbits - 1 - i))
            c = count32(lambda blk, base: (blk == thr) & (crow + base < cand))
            return jnp.where(c < want, cand, m)

        m = lax.fori_loop(0, nbits, tie_step, jnp.zeros((1, tq), I32))
        tie_ref[...] = jnp.where(need, m, tie_ref[...])

    tie_max = tie_ref[...]

    n_far = jnp.clip((q0 - (LANES - 1)) // lt, 0, n_kt)

    def logits_span(base, tiles, near=False):
        rows = tiles * lt
        blk = sc_ref[pl.ds(base, rows), :]
        key_pos = lax.broadcasted_iota(I32, (rows, tq), 0) + base
        sel = (blk > thr) | ((blk == thr) & (key_pos <= tie_max))
        mask = jnp.where(sel, 0.0, MASKED)
        s = jnp.dot(kb_ref[pl.ds(base, rows), :], qt_ref[...], preferred_element_type=F32)
        first_table = (base - q0) // LANES + 2
        tops = []
        for h in range(N_HEADS):
            sh = s[:, h * tq:(h + 1) * tq] + mask
            if near:
                sh = sh + jnp.concatenate([tbl_ref[first_table + 2 * t, h] for t in range(tiles)], axis=0)
            sh = sh * LOG2_E
            s_ref[pl.ds(base, rows), h * tq:(h + 1) * tq] = sh
            tops.append(jnp.max(sh.reshape(rows // SUBLANES, SUBLANES, tq), axis=0))
        mx_ref[...] = jnp.maximum(mx_ref[...], jnp.concatenate(tops, axis=1))

    mx_ref[...] = jnp.full(mx_ref.shape, MASKED * LOG2_E, F32)
    for_tiles(0, n_far, logits_span)
    for tiles in (1, 2):
        pl.when(n_kt - n_far == tiles)(functools.partial(
            logits_span, pl.multiple_of(n_far * lt, lt), tiles, near=True))
    m_fin = jnp.max(mx_ref[...], axis=0, keepdims=True)

    acc_ref[...] = jnp.zeros(acc_ref.shape, F32)
    den_ref[...] = jnp.zeros(den_ref.shape, F32)

    def pv_span(base, tiles):
        rows = tiles * lt
        p = jnp.exp2(s_ref[pl.ds(base, rows), :] - m_fin)
        kt = base // lt
        values_t = jnp.concatenate([vt_ref[kt + t] for t in range(tiles)], axis=1)
        acc_ref[...] += jnp.dot(values_t, p.astype(BF16), preferred_element_type=F32)
        den_ref[...] += jnp.sum(p.reshape(rows // SUBLANES, SUBLANES, N_HEADS * tq), axis=0)

    for_tiles(0, n_kt, pv_span)
    out_t = acc_ref[...] / jnp.sum(den_ref[...], axis=0, keepdims=True)
    lo_lanes = lax.broadcasted_iota(I32, (tq, LANES), 1) < HEAD_DIM
    for c in range(N_HEADS // 2):
        lo = out_t[:, (2 * c) * tq:(2 * c + 1) * tq].T
        hi = out_t[:, (2 * c + 1) * tq:(2 * c + 2) * tq].T
        o_ref[:, c * LANES:(c + 1) * LANES] = jnp.where(lo_lanes, lo, hi).astype(o_ref.dtype)


def _attention(rel_bias, q, qi, kw, kk, kb, vb, *, past, n_keys):
    b, tq_total, _ = q.shape
    lp = kk.shape[1]
    top_k = min(TOPK_MAX, n_keys // 4)
    assert tq_total % Q_TILE == 0 and lp % KEY_TILE == 0 and past % KEY_TILE == 0
    assert lp >= n_keys and top_k <= KEY_TILE
    n_kt = lp // KEY_TILE
    lc = _round_up(lp, SEL_CHUNK)
    vt = vb.reshape(b, n_kt, KEY_TILE, LANES).swapaxes(2, 3)
    bkt = jnp.asarray(_bucket_tables())
    qspec = lambda w: pl.BlockSpec((None, Q_TILE, w), lambda bi, j: (bi, j, 0))
    kspec = pl.BlockSpec((None, lp, LANES), lambda bi, j: (bi, 0, 0))
    return pl.pallas_call(
        functools.partial(_attn_kernel, past=past, n_keys=n_keys, top_k=top_k),
        grid=(b, tq_total // Q_TILE),
        in_specs=[pl.BlockSpec(memory_space=pltpu.SMEM),
                  pl.BlockSpec((3, KEY_TILE, Q_TILE), lambda bi, j: (0, 0, 0)),
                  qspec(ATTN_WIDTH), qspec(N_IDX_HEADS * IDX_DIM), qspec(LANES),
                  kspec, kspec,
                  pl.BlockSpec((None, n_kt, LANES, KEY_TILE), lambda bi, j: (bi, 0, 0, 0))],
        out_specs=qspec(ATTN_WIDTH),
        out_shape=jax.ShapeDtypeStruct((b, tq_total, ATTN_WIDTH), BF16),
        scratch_shapes=[
            pltpu.VMEM((lc, Q_TILE), F32),
            pltpu.VMEM((LANES, N_IDX_HEADS * Q_TILE), BF16),
            pltpu.VMEM((LANES, N_HEADS * Q_TILE), BF16),
            pltpu.VMEM((3, N_HEADS, KEY_TILE, Q_TILE), F32),
            pltpu.VMEM((lp, N_HEADS * Q_TILE), F32),
            pltpu.VMEM((SUBLANES, N_HEADS * Q_TILE), F32),
            pltpu.VMEM((SUBLANES, N_HEADS * Q_TILE), F32),
            pltpu.VMEM((LANES, N_HEADS * Q_TILE), F32),
            pltpu.VMEM((1, Q_TILE), F32),
            pltpu.VMEM((1, Q_TILE), I32),
            pltpu.VMEM((1, Q_TILE), I32),
            pltpu.VMEM((1, Q_TILE), I32),
        ],
        compiler_params=pltpu.CompilerParams(dimension_semantics=("arbitrary", "arbitrary"),
                                             vmem_limit_bytes=VMEM_LIMIT_BYTES),
        name="attn",
    )(rel_bias, bkt, q, qi, kw, kk, kb, vt)


def _sigmoid(x):
    return 0.5 * (jnp.tanh(0.5 * x) + 1.0)


def _softplus(x):
    return jnp.maximum(x, 0.0) + jnp.log1p(jnp.exp(-jnp.abs(x)))


def _rglru_kernel(xr_ref, gr_ref, h0_ref, c0_ref, cw_ref, cb_ref, wa_ref, ba_ref, wx_ref, bx_ref,
                  lam_ref, rnn_ref, hl_ref, xs_ref, a_ref, u_ref, hs_ref, hc_ref, *, tm):
    t = pl.program_id(1)
    pad = SUBLANES
    hist = RNN_CONV - 1

    @pl.when(t == 0)
    def _():
        xs_ref[0:pad, :] = jnp.zeros((pad, xs_ref.shape[-1]), F32)
        xs_ref[pad - hist:pad, :] = c0_ref[...]
        hc_ref[...] = jnp.broadcast_to(h0_ref[...], hc_ref.shape)

    xs_ref[pad:pad + tm, :] = xr_ref[...]
    groups = (tm // SUBLANES, SUBLANES, xs_ref.shape[-1])
    xe = xs_ref[...].reshape((groups[0] + 1,) + groups[1:])
    row = lax.broadcasted_iota(I32, groups, 1)
    xc = cb_ref[...]
    for jj in range(RNN_CONV):
        k = hist - jj
        if k:
            rolled = pltpu.roll(xe, k, 1)
            delayed = jnp.where(row < k, rolled[:-1], rolled[1:])
        else:
            delayed = xe[1:]
        xc = xc + cw_ref[jj:jj + 1, :] * delayed
    xc = xc.reshape(tm, -1)
    xs_ref[0:pad, :] = xs_ref[tm:tm + pad, :]

    xcb = xc.astype(BF16)
    rg = _sigmoid(jnp.dot(xcb, wa_ref[...], preferred_element_type=F32) + ba_ref[...])
    ig = _sigmoid(jnp.dot(xcb, wx_ref[...], preferred_element_type=F32) + bx_ref[...])
    log_a = (-RGLRU_C * rg) * _softplus(-lam_ref[...])
    a = jnp.exp(log_a)
    z = -jnp.tanh(log_a) * (a * a + 1.0)
    u = jnp.where(z > 0.0, z * lax.rsqrt(z), 0.0) * (ig * xc)

    groups = (tm // SUBLANES, SUBLANES, a.shape[-1])
    a = a.reshape(groups)
    u = u.reshape(groups)
    row = lax.broadcasted_iota(I32, groups, 1)
    for s in (1, 2, 4):
        a_prev = pltpu.roll(a, s, 1)
        u_prev = pltpu.roll(u, s, 1)
        use = row >= s
        u = jnp.where(use, a * u_prev + u, u)
        a = jnp.where(use, a * a_prev, a)
    a_ref[...] = a.reshape(tm, -1)
    u_ref[...] = u.reshape(tm, -1)

    def carry_body(g, h):
        base = pl.multiple_of(g * SUBLANES, SUBLANES)
        hg = a_ref[pl.ds(base, SUBLANES), :] * h + u_ref[pl.ds(base, SUBLANES), :]
        hs_ref[pl.ds(base, SUBLANES), :] = hg
        return jnp.broadcast_to(hg[SUBLANES - 1:SUBLANES, :], hg.shape)

    h_end = lax.fori_loop(0, tm // SUBLANES, carry_body, hc_ref[...])
    hc_ref[...] = h_end
    rnn_ref[...] = (hs_ref[...] * jax.nn.gelu(gr_ref[...])).astype(rnn_ref.dtype)

    @pl.when(t == pl.num_programs(1) - 1)
    def _():
        hl_ref[...] = h_end[0:1, :]


def _rglru(xr, gr, h0, conv0, conv_w, conv_b, wa_bd, ba, wx_bd, bx, lam, tm):
    b, t, r = xr.shape
    assert t % tm == 0 and tm % SUBLANES == 0
    tile = pl.BlockSpec((None, tm, r), lambda bi, ti: (bi, ti, 0))
    per_b = lambda rows: pl.BlockSpec((None, rows, r), lambda bi, ti: (bi, 0, 0))
    full = lambda a: pl.BlockSpec(a.shape, lambda bi, ti: (0,) * a.ndim)
    return pl.pallas_call(
        functools.partial(_rglru_kernel, tm=tm),
        grid=(b, t // tm),
        in_specs=[tile, tile, per_b(1), per_b(RNN_CONV - 1), full(conv_w), full(conv_b),
                  full(wa_bd), full(ba), full(wx_bd), full(bx), full(lam)],
        out_specs=(tile, per_b(1)),
        out_shape=(jax.ShapeDtypeStruct((b, t, r), BF16), jax.ShapeDtypeStruct((b, 1, r), F32)),
        scratch_shapes=[pltpu.VMEM((tm + SUBLANES, r), F32), pltpu.VMEM((tm, r), F32),
                        pltpu.VMEM((tm, r), F32), pltpu.VMEM((tm, r), F32),
                        pltpu.VMEM((SUBLANES, r), F32)],
        compiler_params=pltpu.CompilerParams(dimension_semantics=("arbitrary", "arbitrary"),
                                             vmem_limit_bytes=VMEM_LIMIT_BYTES),
        name="rglru",
    )(xr, gr, h0, conv0, conv_w, conv_b, wa_bd, ba, wx_bd, bx, lam)


def _ffn_kernel(x_ref, at_ref, rn_ref, c0_ref, woa_ref, wor_ref, gf_ref, wu_ref, wg_ref, cw_ref,
                cb_ref, wd_ref, gl_ref, y_ref, fc_ref, up_ref, *, tm, final_norm):
    t = pl.program_id(1)
    pad = SUBLANES
    hist = FFN_CONV - 1

    @pl.when(t == 0)
    def _():
        up_ref[pad - hist:pad, :] = c0_ref[...]

    x1 = (x_ref[...]
          + jnp.dot(at_ref[...], woa_ref[...], preferred_element_type=F32)
          + jnp.dot(rn_ref[...], wor_ref[...], preferred_element_type=F32))
    f = _rmsnorm(x1, gf_ref[...]).astype(BF16)
    up_ref[pad:pad + tm, :] = jnp.dot(f, wu_ref[...], preferred_element_type=F32)
    up = cb_ref[...] + cw_ref[0:1, :] * up_ref[pad - hist:pad - hist + tm, :]
    for jj in range(1, FFN_CONV):
        up = up + cw_ref[jj:jj + 1, :] * up_ref[pad - hist + jj:pad - hist + jj + tm, :]
    tail = up_ref[pad + tm - hist:pad + tm, :]
    up_ref[pad - hist:pad, :] = tail
    gate = jnp.dot(f, wg_ref[...], preferred_element_type=F32)
    act = (jax.nn.gelu(up) * gate).astype(BF16)
    x2 = x1 + jnp.dot(act, wd_ref[...], preferred_element_type=F32)
    y_ref[...] = _rmsnorm(x2, gl_ref[...]) if final_norm else x2

    @pl.when(t == pl.num_programs(1) - 1)
    def _():
        fc_ref[...] = tail


def _ffn(x, attn, rnn, conv0, wo_a, wo_r, g_ffn, w_up, w_gate, conv_w, conv_b, w_down, g_last,
         tm, final_norm):
    b, t, d = x.shape
    dff = w_up.shape[1]
    assert t % tm == 0
    tile = lambda w: pl.BlockSpec((None, tm, w), lambda bi, ti: (bi, ti, 0))
    per_b = pl.BlockSpec((None, FFN_CONV - 1, dff), lambda bi, ti: (bi, 0, 0))
    full = lambda a: pl.BlockSpec(a.shape, lambda bi, ti: (0,) * a.ndim,
                                  pipeline_mode=pl.Buffered(1))
    return pl.pallas_call(
        functools.partial(_ffn_kernel, tm=tm, final_norm=final_norm),
        grid=(b, t // tm),
        in_specs=[tile(d), tile(attn.shape[-1]), tile(rnn.shape[-1]), per_b, full(wo_a), full(wo_r),
                  full(g_ffn), full(w_up), full(w_gate), full(conv_w), full(conv_b), full(w_down),
                  full(g_last)],
        out_specs=(tile(d), per_b),
        out_shape=(jax.ShapeDtypeStruct((b, t, d), F32),
                   jax.ShapeDtypeStruct((b, FFN_CONV - 1, dff), F32)),
        scratch_shapes=[pltpu.VMEM((tm + SUBLANES, dff), F32)],
        compiler_params=pltpu.CompilerParams(dimension_semantics=("arbitrary", "arbitrary"),
                                             vmem_limit_bytes=VMEM_LIMIT_BYTES),
        name="ffn",
    )(x, attn, rnn, conv0, wo_a, wo_r, g_ffn, w_up, w_gate, conv_w, conv_b, w_down, g_last)


_HEAD_ORDER = tuple(h for c in range(GQA_GROUP) for h in (c, GQA_GROUP + c))


def _pack_layer_weights(w_in, wa, wx, w_out, w_up, w_gate, w_down):
    d = w_in.shape[0]
    sizes = [ATTN_WIDTH, N_KV_HEADS * HEAD_DIM, N_KV_HEADS * HEAD_DIM, N_IDX_HEADS * IDX_DIM,
             IDX_DIM, N_IDX_HEADS]
    offs = np.cumsum([0] + sizes)
    wq, wk, wv, wqi, wki, wwi = (w_in[:, offs[i]:offs[i + 1]] for i in range(6))
    wrest = w_in[:, offs[6]:]
    rnn_width = wrest.shape[1] // 2
    order = np.asarray(_HEAD_ORDER)
    wq = wq.reshape(d, N_HEADS, HEAD_DIM)[:, order, :].reshape(d, ATTN_WIDTH)
    zpad = jnp.zeros((d, LANES - IDX_DIM - N_IDX_HEADS), w_in.dtype)
    packed = jnp.concatenate([wq, wk, wv, wqi, wki, wwi, zpad, wki, wki, wrest], axis=1).astype(BF16)
    eye = jnp.eye(RNN_BLOCKS, dtype=wa.dtype)
    bd = lambda w: jnp.einsum("nij,nm->nimj", w, eye).reshape(rnn_width, rnn_width).astype(BF16)
    wo_a = w_out[:ATTN_WIDTH].reshape(N_HEADS, HEAD_DIM, -1)[order].reshape(ATTN_WIDTH, -1)
    wo_r = w_out[ATTN_WIDTH:]
    return dict(w_packed=packed, rnn_width=rnn_width, wa_bd=bd(wa), wx_bd=bd(wx),
                wo_a=wo_a.astype(BF16), wo_r=wo_r.astype(BF16), w_up=w_up.astype(BF16),
                w_gate=w_gate.astype(BF16), w_down=w_down.astype(BF16))


def _pad_rows(a, rows):
    return a if a.shape[1] == rows else jnp.pad(a, ((0, 0), (0, rows - a.shape[1]), (0, 0)))


def _round_up(n, m):
    return -(-n // m) * m


def _layer(x, ck, cv, ckidx, h0, rconv0, fconv0, pw, norm_mix, conv_w, conv_b, ba, bx, lam,
           rel_bias, norm_ffn, fconv_w, fconv_b, norm_last, final_norm):
    b, t, d = x.shape
    past = ck.shape[1]
    r = pw["rnn_width"]
    n_tok = b * t
    tm = min(512, n_tok)
    (q, k, v, kb, vb, qi, ki, kk, kw, xr, gr) = _project(
        x.reshape(n_tok, d), norm_mix[None, :], pw["w_packed"], r, tm)
    b3 = lambda a: a.reshape(b, t, a.shape[-1])

    n_keys = past + t
    lp = _round_up(n_keys, KEY_TILE)
    tq = _round_up(t, Q_TILE)
    if past:
        ck2 = ck.reshape(b, past, -1).astype(BF16)
        cv2 = cv.reshape(b, past, -1).astype(BF16)
        cki = ckidx.astype(BF16)
        keys_k = jnp.concatenate([ck2, b3(kb)], axis=1)
        keys_v = jnp.concatenate([cv2, b3(vb)], axis=1)
        keys_i = jnp.concatenate([jnp.concatenate([cki, cki], axis=-1), b3(kk)], axis=1)
    else:
        keys_k, keys_v, keys_i = b3(kb), b3(vb), b3(kk)
    attn = _attention(rel_bias, _pad_rows(b3(q), tq), _pad_rows(b3(qi), tq), _pad_rows(b3(kw), tq),
                      _pad_rows(keys_i, lp), _pad_rows(keys_k, lp), _pad_rows(keys_v, lp),
                      past=past, n_keys=n_keys)[:, :t]

    tr = min(512, t)
    rnn, h_last = _rglru(b3(xr), b3(gr), h0[:, None, :], rconv0, conv_w, conv_b[None, :],
                         pw["wa_bd"], ba[None, :], pw["wx_bd"], bx[None, :], lam[None, :], tr)
    assert t >= RNN_CONV - 1 and t >= FFN_CONV - 1
    rconv_new = b3(xr)[:, t - (RNN_CONV - 1):, :]

    tf = min(512, t)
    y, fconv_new = _ffn(x, attn, rnn, fconv0, pw["wo_a"], pw["wo_r"], norm_ffn[None, :], pw["w_up"],
                        pw["w_gate"], fconv_w, fconv_b[None, :], pw["w_down"], norm_last[None, :],
                        tf, final_norm)
    k_new = k.reshape(b, t, N_KV_HEADS, HEAD_DIM)
    v_new = v.reshape(b, t, N_KV_HEADS, HEAD_DIM)
    return y, (k_new, v_new, b3(ki), h_last[:, 0, :], rconv_new, fconv_new)


def kernel(x_prompt, x_sample, cache_k, cache_v, cache_kidx, state_rglru_h, state_rglru_conv,
           state_ffn_conv, norm_mix, w_in, rglru_conv_w, rglru_conv_b, rglru_wa, rglru_ba,
           rglru_wx, rglru_bx, rglru_lambda, rel_bias, w_out, norm_ffn, w_ffn_up, w_ffn_gate,
           ffn_conv_w, ffn_conv_b, w_ffn_down, norm_final):
    depth = w_in.shape[0]
    bp = x_prompt.shape[0]
    dt = x_prompt.dtype
    xp, xs = x_prompt, x_sample
    outs_p, outs_s = [], []
    for i in range(depth):
        pw = _pack_layer_weights(w_in[i], rglru_wa[i], rglru_wx[i], w_out[i], w_ffn_up[i],
                                 w_ffn_gate[i], w_ffn_down[i])
        r = pw["rnn_width"]
        last = i == depth - 1
        lw = (pw, norm_mix[i], rglru_conv_w[i], rglru_conv_b[i], rglru_ba[i], rglru_bx[i],
              rglru_lambda[i], rel_bias, norm_ffn[i], ffn_conv_w[i], ffn_conv_b[i], norm_final, last)
        xp, st_p = _layer(xp,
                          jnp.zeros((bp, 0, N_KV_HEADS, HEAD_DIM), dt),
                          jnp.zeros((bp, 0, N_KV_HEADS, HEAD_DIM), dt),
                          jnp.zeros((bp, 0, IDX_DIM), dt),
                          jnp.zeros((bp, r), dt),
                          jnp.zeros((bp, RNN_CONV - 1, r), dt),
                          jnp.zeros((bp, FFN_CONV - 1, w_ffn_up.shape[-1]), dt),
                          *lw)
        xs, st_s = _layer(xs, cache_k[i], cache_v[i], cache_kidx[i], state_rglru_h[i],
                          state_rglru_conv[i], state_ffn_conv[i], *lw)
        outs_p.append(st_p)
        outs_s.append(st_s)
    stack = lambda outs, j: jnp.stack([o[j] for o in outs])
    return ((xp, xs) + tuple(stack(outs_p, j) for j in range(6))
            + tuple(stack(outs_s, j) for j in range(6)))
```
